```python
import math
import jax, jax.numpy as jnp
from jax import lax
import numpy as np

D_MODEL = 1024
BATCH = 4
SEQ = 8192
DEPTH = 4

D_FF = 2816
PLE_DIM = 256
CONV_WIDTH = 3
CONV_CH = 512
S5_CH = 512
S5_GROUP = 16
S5_GROUPS = S5_CH // S5_GROUP
S5_STATE = 64
AB_IN = 3 * CONV_CH + S5_CH
N_HEADS = 8
HEAD_DIM = 128
N_KV_HEADS = 2
IDX_HEADS = 8
IDX_DIM = 64
TOPK_MAX = 256
Q_BLOCK = 128
C_SIZES = (N_HEADS * HEAD_DIM, N_KV_HEADS * HEAD_DIM, N_KV_HEADS * HEAD_DIM,
           IDX_HEADS * IDX_DIM, IDX_DIM, IDX_HEADS)
C_SPLITS = tuple(int(v) for v in np.cumsum(C_SIZES)[:-1])
C_IN = int(sum(C_SIZES))
ROPE_THETA = 500000.0
ROT_FRAC = 4
LN_EPS = 1e-5
DN_ALPHA = (2 * DEPTH) ** 0.25
DN_BETA = (8 * DEPTH) ** -0.25
N_EVEN = (DEPTH + 1) // 2
N_ODD = DEPTH // 2

kernel_name = 'hybrid_conv_s5_dsa_macaron_deepnorm'


def layer_norm(x, g, b):
    xf = x.astype(jnp.float32)
    mu = jnp.mean(xf, axis=-1, keepdims=True)
    var = jnp.mean(jnp.square(xf - mu), axis=-1, keepdims=True)
    return ((xf - mu) * lax.rsqrt(var + LN_EPS) * g.astype(jnp.float32) + b.astype(jnp.float32)).astype(x.dtype)


def swiglu(x, w1, w3, w2):
    return (jax.nn.silu(x @ w1) * (x @ w3)) @ w2


def rope_tables(positions, rot_dim):
    inv = ROPE_THETA ** (-jnp.arange(0, rot_dim, 2, dtype=jnp.float32) / rot_dim)
    ang = positions.astype(jnp.float32)[..., None] * inv
    return jnp.cos(ang), jnp.sin(ang)


def partial_rope(t, cos, sin):
    half = cos.shape[-1]
    r = 2 * half
    tr = t[..., :r].astype(jnp.float32)
    t1, t2 = tr[..., :half], tr[..., half:]
    rot = jnp.concatenate([t1 * cos - t2 * sin, t2 * cos + t1 * sin], axis=-1).astype(t.dtype)
    return jnp.concatenate([rot, t[..., r:]], axis=-1)


def short_conv_mixer(h, gb, gc, conv_w):
    u = gc * h
    L = u.shape[1]
    up = jnp.pad(u, ((0, 0), (CONV_WIDTH - 1, 0), (0, 0)))
    v = sum(conv_w[j] * up[:, j:j + L] for j in range(CONV_WIDTH))
    return gb * v


def s5_mixer(u, lam_re, lam_im, log_dt, b_re, b_im, c_re, c_im, d_skip, w_glu, b_glu):
    f32 = jnp.float32
    Bt, L, _ = u.shape
    uf = u.astype(f32)
    ug = uf.reshape(Bt, L, S5_GROUPS, S5_GROUP)
    lr = jnp.minimum(lam_re.astype(f32), -1e-4)
    li = lam_im.astype(f32)
    dt = jnp.exp(log_dt.astype(f32))[:, None]
    mag = jnp.exp(lr * dt)
    ab_re = mag * jnp.cos(li * dt)
    ab_im = mag * jnp.sin(li * dt)
    nr, ni = ab_re - 1.0, ab_im
    den = lr * lr + li * li
    f_re = (nr * lr + ni * li) / den
    f_im = (ni * lr - nr * li) / den
    br, bi = b_re.astype(f32), b_im.astype(f32)
    bb_re = f_re[..., None] * br - f_im[..., None] * bi
    bb_im = f_re[..., None] * bi + f_im[..., None] * br
    bu_re = jnp.einsum('blgc,gpc->blgp', ug, bb_re)
    bu_im = jnp.einsum('blgc,gpc->blgp', ug, bb_im)
    a_re = jnp.broadcast_to(ab_re, bu_re.shape)
    a_im = jnp.broadcast_to(ab_im, bu_im.shape)

    def combine(e1, e2):
        a1r, a1i, b1r, b1i = e1
        a2r, a2i, b2r, b2i = e2
        return (a2r * a1r - a2i * a1i, a2r * a1i + a2i * a1r,
                a2r * b1r - a2i * b1i + b2r, a2r * b1i + a2i * b1r + b2i)

    _, _, s_re, s_im = lax.associative_scan(combine, (a_re, a_im, bu_re, bu_im), axis=1)
    y = (jnp.einsum('blgp,gcp->blgc', s_re, c_re.astype(f32))
         - jnp.einsum('blgp,gcp->blgc', s_im, c_im.astype(f32)))
    y = y.reshape(Bt, L, S5_CH) + d_skip.astype(f32) * uf
    z = jax.nn.gelu(y)
    out = z * jax.nn.sigmoid(z @ w_glu.astype(f32) + b_glu.astype(f32))
    return out.astype(u.dtype)


def dsa_mixer(proj, cos_a, sin_a, cos_i, sin_i):
    f32 = jnp.float32
    Bt, L, _ = proj.shape
    q, k, v, qi, ki, wi = jnp.split(proj, C_SPLITS, axis=-1)
    q = partial_rope(q.reshape(Bt, L, N_HEADS, HEAD_DIM), cos_a[:, :, None], sin_a[:, :, None])
    k = partial_rope(k.reshape(Bt, L, N_KV_HEADS, HEAD_DIM), cos_a[:, :, None], sin_a[:, :, None])
    v = v.reshape(Bt, L, N_KV_HEADS, HEAD_DIM)
    qi = partial_rope(qi.reshape(Bt, L, IDX_HEADS, IDX_DIM), cos_i[:, :, None], sin_i[:, :, None])
    ki = partial_rope(ki, cos_i, sin_i).astype(f32)
    wi = wi.astype(f32) * (IDX_HEADS ** -0.5 * IDX_DIM ** -0.5)
    topk = min(TOPK_MAX, L // 4)
    nb = L // Q_BLOCK

    def to_blocks(t):
        return jnp.moveaxis(t.reshape(Bt, nb, Q_BLOCK, *t.shape[2:]), 1, 0)

    t_pos = jnp.arange(L, dtype=jnp.int32).reshape(nb, Q_BLOCK)
    key_pos = jnp.arange(L, dtype=jnp.int32)
    bidx = jnp.arange(Bt)[:, None, None]
    rep = N_HEADS // N_KV_HEADS

    def block(args):
        qb, qib, wb, tb = args
        logits = jax.nn.relu(jnp.einsum('bthd,bsd->bths', qib.astype(f32), ki))
        score = jnp.einsum('bths,bth->bts', logits, wb)
        causal = key_pos[None, :] <= tb[:, None]
        score = jnp.where(causal[None], score, -jnp.inf)
        _, idx = lax.top_k(score, topk)
        valid = idx <= tb[None, :, None]
        k_sel = k[bidx, idx]
        v_sel = v[bidx, idx]
        qg = qb.reshape(Bt, Q_BLOCK, N_KV_HEADS, rep, HEAD_DIM)
        s = jnp.einsum('btgrd,btjgd->btgrj', qg, k_sel).astype(f32) * (HEAD_DIM ** -0.5)
        s = jnp.where(valid[:, :, None, None, :], s, -jnp.inf)
        pr = jax.nn.softmax(s, axis=-1)
        o = jnp.einsum('btgrj,btjgd->btgrd', pr.astype(v.dtype), v_sel)
        return o.reshape(Bt, Q_BLOCK, N_HEADS * HEAD_DIM)

    out = lax.map(block, (to_blocks(q), to_blocks(qi), to_blocks(wi), t_pos))
    return jnp.moveaxis(out, 0, 1).reshape(Bt, L, N_HEADS * HEAD_DIM)


def setup_inputs(seed: int = 0) -> dict:
    key = jax.random.key(seed)
    ks = jax.random.split(key, 32)
    f32 = jnp.float32

    def nrm(i, shape, scale):
        return jax.random.normal(ks[i], shape, f32) * scale

    x = nrm(0, (BATCH, SEQ, D_MODEL), 1.0)
    p = nrm(1, (DEPTH, BATCH, SEQ, PLE_DIM), 1.0)
    offs = jax.random.randint(ks[2], (BATCH, 1), 0, 1024, dtype=jnp.int32)
    positions = offs + jnp.arange(SEQ, dtype=jnp.int32)[None, :]
    ln_g = 1.0 + nrm(3, (DEPTH, 3, D_MODEL), 0.02)
    ln_b = nrm(4, (DEPTH, 3, D_MODEL), 0.02)
    ffn_w1 = nrm(5, (DEPTH, 2, D_MODEL, D_FF), D_MODEL ** -0.5)
    ffn_w3 = nrm(6, (DEPTH, 2, D_MODEL, D_FF), D_MODEL ** -0.5)
    ffn_w2 = nrm(7, (DEPTH, 2, D_FF, D_MODEL), D_FF ** -0.5 * DN_BETA)
    ple_w_proj = nrm(8, (DEPTH, PLE_DIM, D_MODEL), PLE_DIM ** -0.5)
    ple_w_gate = nrm(9, (DEPTH, D_MODEL, D_MODEL), D_MODEL ** -0.5)
    ab_w_in = nrm(10, (N_EVEN, D_MODEL, AB_IN), D_MODEL ** -0.5)
    ab_w_out = nrm(11, (N_EVEN, CONV_CH + S5_CH, D_MODEL), (CONV_CH + S5_CH) ** -0.5 * DN_BETA)
    conv_w = nrm(12, (N_EVEN, CONV_WIDTH, CONV_CH), CONV_WIDTH ** -0.5)
    s5_lam_re = -0.5 + nrm(13, (N_EVEN, S5_GROUPS, S5_STATE), 0.01)
    s5_lam_im = (math.pi * jnp.arange(S5_STATE, dtype=f32))[None, None, :] + nrm(14, (N_EVEN, S5_GROUPS, S5_STATE), 0.01)
    s5_log_dt = jax.random.uniform(ks[15], (N_EVEN, S5_GROUPS), f32, math.log(1e-3), math.log(1e-1))
    s5_b_re = nrm(16, (N_EVEN, S5_GROUPS, S5_STATE, S5_GROUP), (2 * S5_GROUP) ** -0.5)
    s5_b_im = nrm(17, (N_EVEN, S5_GROUPS, S5_STATE, S5_GROUP), (2 * S5_GROUP) ** -0.5)
    s5_c_re = nrm(18, (N_EVEN, S5_GROUPS, S5_GROUP, S5_STATE), S5_STATE ** -0.5)
    s5_c_im = nrm(19, (N_EVEN, S5_GROUPS, S5_GROUP, S5_STATE), S5_STATE ** -0.5)
    s5_d = nrm(20, (N_EVEN, S5_CH), 1.0)
    s5_w_glu = nrm(21, (N_EVEN, S5_CH, S5_CH), S5_CH ** -0.5)
    s5_b_glu = nrm(22, (N_EVEN, S5_CH), 0.02)
    c_w_in = nrm(23, (N_ODD, D_MODEL, C_IN), D_MODEL ** -0.5)
    c_w_out = nrm(24, (N_ODD, N_HEADS * HEAD_DIM, D_MODEL), (N_HEADS * HEAD_DIM) ** -0.5 * DN_BETA)
    return {'x': x, 'p': p, 'positions': positions, 'ln_g': ln_g, 'ln_b': ln_b,
            'ffn_w1': ffn_w1, 'ffn_w3': ffn_w3, 'ffn_w2': ffn_w2,
            'ple_w_proj': ple_w_proj, 'ple_w_gate': ple_w_gate,
            'ab_w_in': ab_w_in, 'ab_w_out': ab_w_out, 'conv_w': conv_w,
            's5_lam_re': s5_lam_re, 's5_lam_im': s5_lam_im, 's5_log_dt': s5_log_dt,
            's5_b_re': s5_b_re, 's5_b_im': s5_b_im, 's5_c_re': s5_c_re, 's5_c_im': s5_c_im,
            's5_d': s5_d, 's5_w_glu': s5_w_glu, 's5_b_glu': s5_b_glu,
            'c_w_in': c_w_in, 'c_w_out': c_w_out}


def reference(x, p, positions, ln_g, ln_b, ffn_w1, ffn_w3, ffn_w2, ple_w_proj, ple_w_gate,
              ab_w_in, ab_w_out, conv_w, s5_lam_re, s5_lam_im, s5_log_dt, s5_b_re, s5_b_im,
              s5_c_re, s5_c_im, s5_d, s5_w_glu, s5_b_glu, c_w_in, c_w_out):
    cos_a, sin_a = rope_tables(positions, HEAD_DIM // ROT_FRAC)
    cos_i, sin_i = rope_tables(positions, IDX_DIM // ROT_FRAC)
    h = x
    for i in range(DEPTH):
        h = layer_norm(DN_ALPHA * h + 0.5 * swiglu(h, ffn_w1[i, 0], ffn_w3[i, 0], ffn_w2[i, 0]),
                       ln_g[i, 0], ln_b[i, 0])
        j = i // 2
        if i % 2 == 0:
            proj = h @ ab_w_in[j]
            hc, gb, gc, u = jnp.split(proj, [CONV_CH, 2 * CONV_CH, 3 * CONV_CH], axis=-1)
            ya = short_conv_mixer(hc, gb, gc, conv_w[j])
            yb = s5_mixer(u, s5_lam_re[j], s5_lam_im[j], s5_log_dt[j], s5_b_re[j], s5_b_im[j],
                          s5_c_re[j], s5_c_im[j], s5_d[j], s5_w_glu[j], s5_b_glu[j])
            mix = jnp.concatenate([ya, yb], axis=-1) @ ab_w_out[j]
        else:
            proj = h @ c_w_in[j]
            mix = dsa_mixer(proj, cos_a, sin_a, cos_i, sin_i) @ c_w_out[j]
        h = layer_norm(DN_ALPHA * h + mix, ln_g[i, 1], ln_b[i, 1])
        h = layer_norm(DN_ALPHA * h + 0.5 * swiglu(h, ffn_w1[i, 1], ffn_w3[i, 1], ffn_w2[i, 1]),
                       ln_g[i, 2], ln_b[i, 2])
        h = h + (p[i] @ ple_w_proj[i]) * jax.nn.sigmoid(h @ ple_w_gate[i])
    return h
```

```python
import functools
import math

import jax
import jax.numpy as jnp
from jax import lax
from jax.experimental import pallas as pl
from jax.experimental.pallas import tpu as pltpu

F32 = jnp.float32
BF16 = jnp.bfloat16
HIGHEST = lax.Precision.HIGHEST

DEPTH = 4
LN_EPS = 1e-5
DN_ALPHA = (2 * DEPTH) ** 0.25
CONV_WIDTH = 3
CONV_CH = 512
S5_CH = 512
S5_GROUP = 16
S5_GROUPS = S5_CH // S5_GROUP
S5_STATE = 64
N_HEADS = 8
HEAD_DIM = 128
N_KV_HEADS = 2
IDX_HEADS = 8
IDX_DIM = 64
TOPK_MAX = 256
ROPE_THETA = 500000.0
ROT_FRAC = 4

LANES = 128
SUBLANES = 8
VMEM_LIMIT = 56 * 1024 * 1024

FFN_ROWS = 512
FF_CHUNK = 256
PROJ_ROWS = 512
S5_T = 16
DSA_TQ = 128
DSA_TK = 512
KEY_NINF = -0x7F800000


def _cparams(*sem):
    return pltpu.CompilerParams(dimension_semantics=sem, vmem_limit_bytes=VMEM_LIMIT)


def _layer_norm(y, g, b):
    mu = jnp.mean(y, axis=-1, keepdims=True)
    d = y - mu
    var = jnp.mean(d * d, axis=-1, keepdims=True)
    return d * lax.rsqrt(var + LN_EPS) * g + b


def _const_spec(shape):
    nd = len(shape)
    return pl.BlockSpec(shape, lambda *_: (0,) * nd, pipeline_mode=pl.Buffered(1))


def _ffn_body(x_ref, w1_ref, w3_ref, w2_ref, g_ref, b_ref, acc_ref):
    x = x_ref[...]
    xb = x.astype(BF16)
    acc_ref[...] = jnp.zeros_like(acc_ref)

    def step(c, carry):
        a = jnp.dot(xb, w1_ref[c], preferred_element_type=F32)
        gt = jnp.dot(xb, w3_ref[c], preferred_element_type=F32)
        act = (a * jax.nn.sigmoid(a)) * gt
        acc_ref[...] += jnp.dot(act.astype(BF16), w2_ref[c], preferred_element_type=F32)
        return carry

    lax.fori_loop(0, w1_ref.shape[0], step, 0)
    y = DN_ALPHA * x + 0.5 * acc_ref[...]
    return _layer_norm(y, g_ref[...], b_ref[...])


def _ffn_kernel(x_ref, w1_ref, w3_ref, w2_ref, g_ref, b_ref, o_ref, acc_ref):
    o_ref[...] = _ffn_body(x_ref, w1_ref, w3_ref, w2_ref, g_ref, b_ref, acc_ref)


def _ffn_ple_kernel(x_ref, w1_ref, w3_ref, w2_ref, g_ref, b_ref, p_ref, wp_ref, wg_ref,
                    o_ref, acc_ref):
    h = _ffn_body(x_ref, w1_ref, w3_ref, w2_ref, g_ref, b_ref, acc_ref)
    gate = jax.nn.sigmoid(jnp.dot(h.astype(BF16), wg_ref[...], preferred_element_type=F32))
    pe = jnp.dot(p_ref[...].astype(BF16), wp_ref[...], preferred_element_type=F32)
    o_ref[...] = h + pe * gate


def _ffn_ln(x, w1c, w3c, w2c, g, b, ple=None):
    m, d = x.shape
    nc, _, fc = w1c.shape
    row = pl.BlockSpec((FFN_ROWS, d), lambda i: (i, 0))
    in_specs = [row, _const_spec((nc, d, fc)), _const_spec((nc, d, fc)), _const_spec((nc, fc, d)),
                _const_spec((1, d)), _const_spec((1, d))]
    args = [x, w1c, w3c, w2c, g, b]
    body = _ffn_kernel
    name = "ffn_ln"
    if ple is not None:
        p, wp, wg = ple
        in_specs += [pl.BlockSpec((FFN_ROWS, p.shape[1]), lambda i: (i, 0)),
                     _const_spec(wp.shape), _const_spec(wg.shape)]
        args += [p, wp, wg]
        body = _ffn_ple_kernel
        name = "ffn_ln_ple"
    return pl.pallas_call(
        body,
        out_shape=jax.ShapeDtypeStruct((m, d), F32),
        grid=(m // FFN_ROWS,),
        in_specs=in_specs,
        out_specs=row,
        scratch_shapes=[pltpu.VMEM((FFN_ROWS, d), F32)],
        compiler_params=_cparams("parallel"),
        name=name,
    )(*args)


def _ab_proj_kernel(x_ref, w_ref, cg_ref, u_ref):
    y = jnp.dot(x_ref[...].astype(BF16), w_ref[...], preferred_element_type=F32)
    n_cg = cg_ref.shape[1]
    cg_ref[...] = y[:, :n_cg]
    u_ref[...] = y[:, n_cg:]


def _ab_proj(x, w):
    m, d = x.shape
    n = w.shape[1]
    n_cg = 3 * CONV_CH
    return pl.pallas_call(
        _ab_proj_kernel,
        out_shape=(jax.ShapeDtypeStruct((m, n_cg), F32), jax.ShapeDtypeStruct((m, n - n_cg), F32)),
        grid=(m // PROJ_ROWS,),
        in_specs=[pl.BlockSpec((PROJ_ROWS, d), lambda i: (i, 0)), _const_spec((d, n))],
        out_specs=(pl.BlockSpec((PROJ_ROWS, n_cg), lambda i: (i, 0)),
                   pl.BlockSpec((PROJ_ROWS, n - n_cg), lambda i: (i, 0))),
        compiler_params=_cparams("parallel"),
        name="ab_in_proj",
    )(x, w)


def _s5_tables(lam_re, lam_im, log_dt, b_re, b_im, c_re, c_im):
    lr = jnp.minimum(lam_re, -1e-4)
    li = lam_im
    dt = jnp.exp(log_dt)[:, None]
    mag = jnp.exp(lr * dt)
    ab_re = mag * jnp.cos(li * dt)
    ab_im = mag * jnp.sin(li * dt)
    nr, ni = ab_re - 1.0, ab_im
    den = lr * lr + li * li
    f_re = (nr * lr + ni * li) / den
    f_im = (ni * lr - nr * li) / den
    bb_re = f_re[..., None] * b_re - f_im[..., None] * b_im
    bb_im = f_re[..., None] * b_im + f_im[..., None] * b_re
    pw_re = [jnp.ones_like(ab_re)]
    pw_im = [jnp.zeros_like(ab_im)]
    for _ in range(S5_T):
        pr, pi = pw_re[-1], pw_im[-1]
        pw_re.append(pr * ab_re - pi * ab_im)
        pw_im.append(pr * ab_im + pi * ab_re)
    pw_re = jnp.stack(pw_re)
    pw_im = jnp.stack(pw_im)
    cp_re = c_re[None] * pw_re[:, :, None, :] - c_im[None] * pw_im[:, :, None, :]
    cp_im = c_re[None] * pw_im[:, :, None, :] + c_im[None] * pw_re[:, :, None, :]
    taps = (jnp.einsum('ngcp,gpd->ngcd', cp_re[:S5_T], bb_re, precision=HIGHEST)
            - jnp.einsum('ngcp,gpd->ngcd', cp_im[:S5_T], bb_im, precision=HIGHEST))
    s_idx = jnp.arange(S5_T)[:, None]
    t_idx = jnp.arange(S5_T)[None, :]
    lag = t_idx - s_idx
    k_st = taps[jnp.clip(lag, 0, S5_T - 1)]
    k_st = jnp.where((lag >= 0)[:, :, None, None, None], k_st, 0.0)
    toep = jnp.transpose(k_st, (2, 0, 4, 1, 3)).reshape(S5_GROUPS, S5_T * S5_GROUP, S5_T * S5_GROUP)
    rev_re = pw_re[S5_T - 1::-1][:S5_T]
    rev_im = pw_im[S5_T - 1::-1][:S5_T]
    win_re = rev_re[..., None] * bb_re[None] - rev_im[..., None] * bb_im[None]
    win_im = rev_re[..., None] * bb_im[None] + rev_im[..., None] * bb_re[None]
    w_in = jnp.concatenate([win_re, win_im], axis=2)
    w_in = jnp.transpose(w_in, (1, 0, 3, 2)).reshape(S5_GROUPS, S5_T * S5_GROUP, 2 * S5_STATE)
    wo_re = jnp.transpose(cp_re[1:], (1, 3, 0, 2))
    wo_im = jnp.transpose(-cp_im[1:], (1, 3, 0, 2))
    w_out = jnp.concatenate([wo_re, wo_im], axis=1).reshape(S5_GROUPS, 2 * S5_STATE, S5_T * S5_GROUP)
    a1 = jnp.concatenate([pw_re[S5_T], pw_re[S5_T]], axis=-1)
    a2 = jnp.concatenate([-pw_im[S5_T], pw_im[S5_T]], axis=-1)
    return toep, w_in, w_out, a1, a2


def _s5_local_kernel(uc_ref, win_ref, o_ref):
    o_ref[0] = jnp.dot(uc_ref[0], win_ref[0], preferred_element_type=F32, precision=HIGHEST)


def _s5_local(uc, w_in):
    g, rows, width = uc.shape
    ns = w_in.shape[2]
    return pl.pallas_call(
        _s5_local_kernel,
        out_shape=jax.ShapeDtypeStruct((g, rows, ns), F32),
        grid=(g,),
        in_specs=[pl.BlockSpec((1, rows, width), lambda i: (i, 0, 0)),
                  pl.BlockSpec((1, width, ns), lambda i: (i, 0, 0))],
        out_specs=pl.BlockSpec((1, rows, ns), lambda i: (i, 0, 0)),
        compiler_params=_cparams("parallel"),
        name="s5_chunk_state_in",
    )(uc, w_in)


def _s5_scan_kernel(g_ref, a1_ref, a2_ref, o_ref):
    a1 = a1_ref[...]
    a2 = a2_ref[...]
    half = a1.shape[1] // 2

    def step(j, x):
        o_ref[j] = x
        return a1 * x + a2 * pltpu.roll(x, half, 1) + g_ref[j]

    lax.fori_loop(0, g_ref.shape[0], step, jnp.zeros(a1.shape, F32))


def _s5_scan(gl, a1, a2):
    nch, rows, ns = gl.shape
    rb = 32
    return pl.pallas_call(
        _s5_scan_kernel,
        out_shape=jax.ShapeDtypeStruct((nch, rows, ns), F32),
        grid=(rows // rb,),
        in_specs=[pl.BlockSpec((nch, rb, ns), lambda i: (0, i, 0)),
                  pl.BlockSpec((rb, ns), lambda i: (i, 0)),
                  pl.BlockSpec((rb, ns), lambda i: (i, 0))],
        out_specs=pl.BlockSpec((nch, rb, ns), lambda i: (0, i, 0)),
        compiler_params=_cparams("parallel"),
        name="s5_chunk_scan",
    )(gl, a1, a2)


def _s5_out_kernel(uc_ref, xp_ref, toep_ref, wout_ref, o_ref):
    y = jnp.dot(uc_ref[0], toep_ref[0], preferred_element_type=F32, precision=HIGHEST)
    y = y + jnp.dot(xp_ref[0], wout_ref[0], preferred_element_type=F32, precision=HIGHEST)
    o_ref[0] = y


def _s5_out(uc, xprev, toep, w_out):
    g, rows, width = uc.shape
    ns = xprev.shape[2]
    return pl.pallas_call(
        _s5_out_kernel,
        out_shape=jax.ShapeDtypeStruct((g, rows, width), F32),
        grid=(g,),
        in_specs=[pl.BlockSpec((1, rows, width), lambda i: (i, 0, 0)),
                  pl.BlockSpec((1, rows, ns), lambda i: (i, 0, 0)),
                  pl.BlockSpec((1, width, width), lambda i: (i, 0, 0)),
                  pl.BlockSpec((1, ns, width), lambda i: (i, 0, 0))],
        out_specs=pl.BlockSpec((1, rows, width), lambda i: (i, 0, 0)),
        compiler_params=_cparams("parallel"),
        name="s5_chunk_out",
    )(uc, xprev, toep, w_out)


def _s5_linear(u, tables, batch, seq):
    toep, w_in, w_out, a1, a2 = tables
    nch = seq // S5_T
    uc = u.reshape(batch, nch, S5_T, S5_GROUPS, S5_GROUP)
    uc = jnp.transpose(uc, (3, 0, 1, 2, 4)).reshape(S5_GROUPS, batch * nch, S5_T * S5_GROUP)
    gl = _s5_local(uc, w_in)
    ns = gl.shape[2]
    gl = jnp.transpose(gl.reshape(S5_GROUPS, batch, nch, ns), (2, 1, 0, 3))
    gl = gl.reshape(nch, batch * S5_GROUPS, ns)
    xprev = _s5_scan(gl, jnp.tile(a1, (batch, 1)), jnp.tile(a2, (batch, 1)))
    xprev = jnp.transpose(xprev.reshape(nch, batch, S5_GROUPS, ns), (2, 1, 0, 3))
    xprev = xprev.reshape(S5_GROUPS, batch * nch, ns)
    yc = _s5_out(uc, xprev, toep, w_out)
    yc = yc.reshape(S5_GROUPS, batch, nch, S5_T, S5_GROUP)
    return jnp.transpose(yc, (1, 2, 3, 0, 4)).reshape(batch * seq, S5_CH)


def _gelu_tanh(x):
    c = math.sqrt(2.0 / math.pi)
    return 0.5 * x * (1.0 + jnp.tanh(c * (x + 0.044715 * (x * x * x))))


def _ab_mix_kernel(h_ref, cg_ref, u_ref, y_ref, cw_ref, d_ref, wglu_ref, bglu_ref, wout_ref,
                   g_ref, b_ref, o_ref, uext_ref):
    tm = h_ref.shape[1]
    cc = CONV_CH
    halo = SUBLANES

    @pl.when(pl.program_id(1) == 0)
    def _():
        uext_ref[0:halo, :] = jnp.zeros((halo, cc), F32)

    cg = cg_ref[0]
    uc = cg[:, 2 * cc:3 * cc] * cg[:, 0:cc]
    uext_ref[halo:halo + tm, :] = uc
    v = (cw_ref[0:1, :] * uext_ref[halo - 2:halo - 2 + tm, :]
         + cw_ref[1:2, :] * uext_ref[halo - 1:halo - 1 + tm, :]
         + cw_ref[2:3, :] * uc)
    ya = cg[:, cc:2 * cc] * v
    uext_ref[0:halo, :] = uc[tm - halo:tm, :]

    y2 = y_ref[0] + d_ref[...] * u_ref[0]
    z = _gelu_tanh(y2)
    gate = jax.nn.sigmoid(jnp.dot(z.astype(BF16), wglu_ref[...], preferred_element_type=F32)
                          + bglu_ref[...])
    yb = z * gate
    mix = (jnp.dot(ya.astype(BF16), wout_ref[0:cc, :], preferred_element_type=F32)
           + jnp.dot(yb.astype(BF16), wout_ref[cc:, :], preferred_element_type=F32))
    o_ref[0] = _layer_norm(DN_ALPHA * h_ref[0] + mix, g_ref[...], b_ref[...])


def _ab_mix(h, cg, u, y, conv_w, d_skip, w_glu, b_glu, w_out, g, b, batch, seq):
    d = h.shape[-1]
    tm = PROJ_ROWS

    def row(width):
        return pl.BlockSpec((1, tm, width), lambda bi, si: (bi, si, 0))

    r3 = lambda a: a.reshape(batch, seq, a.shape[-1])
    out = pl.pallas_call(
        _ab_mix_kernel,
        out_shape=jax.ShapeDtypeStruct((batch, seq, d), F32),
        grid=(batch, seq // tm),
        in_specs=[row(d), row(3 * CONV_CH), row(S5_CH), row(S5_CH),
                  _const_spec(conv_w.shape), _const_spec(d_skip.shape), _const_spec(w_glu.shape),
                  _const_spec(b_glu.shape), _const_spec(w_out.shape),
                  _const_spec(g.shape), _const_spec(b.shape)],
        out_specs=row(d),
        scratch_shapes=[pltpu.VMEM((tm + SUBLANES, CONV_CH), F32)],
        compiler_params=_cparams("parallel", "arbitrary"),
        name="ab_mix_out_ln",
    )(r3(h), r3(cg), r3(u), r3(y), conv_w, d_skip, w_glu, b_glu, w_out, g, b)
    return out.reshape(batch * seq, d)


def _rope_tables(positions, rot_dim, width, repeat):
    half = rot_dim // 2
    inv = ROPE_THETA ** (-jnp.arange(0, rot_dim, 2, dtype=F32) / rot_dim)
    ang = positions.astype(F32)[..., None] * inv
    cos, sin = jnp.cos(ang), jnp.sin(ang)
    lead = cos.shape[:-1]
    ones = jnp.ones(lead + (width - 2 * half,), F32)
    zeros = jnp.zeros(lead + (width - 2 * half,), F32)
    ct = jnp.concatenate([cos, cos, ones], axis=-1)
    st = jnp.concatenate([-sin, sin, zeros], axis=-1)
    return jnp.tile(ct, (1, 1, repeat)), jnp.tile(st, (1, 1, repeat))


def _rotate(t, cos_t, sin_t, half, period):
    lane = lax.broadcasted_iota(jnp.int32, t.shape, 1)
    first = (lane % period) < half
    partner = jnp.where(first, pltpu.roll(t, LANES - half, 1), pltpu.roll(t, half, 1))
    return t * cos_t + partner * sin_t


def _c_proj_kernel(x_ref, w_ref, ca_ref, sa_ref, ci_ref, si_ref,
                   q_ref, k_ref, v_ref, qi_ref, ki_ref, wi_ref):
    y = jnp.dot(x_ref[...].astype(BF16), w_ref[...], preferred_element_type=F32)
    ca, sa, ci, si = ca_ref[...], sa_ref[...], ci_ref[...], si_ref[...]
    half_a = HEAD_DIM // ROT_FRAC // 2
    half_i = IDX_DIM // ROT_FRAC // 2
    nq = N_HEADS * HEAD_DIM
    nkv = N_KV_HEADS * HEAD_DIM
    nqi = IDX_HEADS * IDX_DIM
    for hd in range(N_HEADS):
        t = y[:, hd * LANES:(hd + 1) * LANES]
        q_ref[:, hd * LANES:(hd + 1) * LANES] = _rotate(t, ca, sa, half_a, HEAD_DIM).astype(BF16)
    for hd in range(N_KV_HEADS):
        t = y[:, nq + hd * LANES:nq + (hd + 1) * LANES]
        k_ref[:, hd * LANES:(hd + 1) * LANES] = _rotate(t, ca, sa, half_a, HEAD_DIM).astype(BF16)
    v_ref[...] = y[:, nq + nkv:nq + 2 * nkv].astype(BF16)
    o = nq + 2 * nkv
    for pr in range(nqi // LANES):
        t = y[:, o + pr * LANES:o + (pr + 1) * LANES]
        qi_ref[:, pr * LANES:(pr + 1) * LANES] = _rotate(t, ci, si, half_i, IDX_DIM).astype(BF16)
    o += nqi
    t = y[:, o:o + LANES]
    lane = lax.broadcasted_iota(jnp.int32, t.shape, 1)
    ki_lo = jnp.where(lane < IDX_DIM, _rotate(t, ci, si, half_i, IDX_DIM), 0.0)
    ki_ref[:, 0:LANES] = ki_lo.astype(BF16)
    ki_ref[:, LANES:2 * LANES] = pltpu.roll(ki_lo, IDX_DIM, 1).astype(BF16)
    wi_ref[...] = t[:, IDX_DIM:IDX_DIM + IDX_HEADS] * (IDX_HEADS ** -0.5 * IDX_DIM ** -0.5)


def _c_proj(x, w, ca, sa, ci, si):
    m, d = x.shape
    n = w.shape[1]
    tm = PROJ_ROWS
    nq = N_HEADS * HEAD_DIM
    nkv = N_KV_HEADS * HEAD_DIM
    nqi = IDX_HEADS * IDX_DIM
    row = lambda width: pl.BlockSpec((tm, width), lambda i: (i, 0))
    return pl.pallas_call(
        _c_proj_kernel,
        out_shape=(jax.ShapeDtypeStruct((m, nq), BF16), jax.ShapeDtypeStruct((m, nkv), BF16),
                   jax.ShapeDtypeStruct((m, nkv), BF16), jax.ShapeDtypeStruct((m, nqi), BF16),
                   jax.ShapeDtypeStruct((m, 2 * LANES), BF16), jax.ShapeDtypeStruct((m, IDX_HEADS), F32)),
        grid=(m // tm,),
        in_specs=[row(d), _const_spec((d, n)), row(LANES), row(LANES), row(LANES), row(LANES)],
        out_specs=(row(nq), row(nkv), row(nkv), row(nqi), row(2 * LANES), row(IDX_HEADS)),
        compiler_params=_cparams("parallel"),
        name="c_in_proj_rope",
    )(x, w, ca, sa, ci, si)


def _ordered_key(s):
    bits = pltpu.bitcast(s, jnp.int32)
    return jnp.where(bits < 0, -(bits ^ jnp.int32(-0x80000000)), bits)


def _dsa_kernel(q_ref, qi_ref, wi_ref, k_ref, v_ref, ki_ref, o_ref, key_ref, acc_ref, jcut_ref,
                *, topk):
    tq, tk = DSA_TQ, DSA_TK
    pos_bits = (k_ref.shape[1] - 1).bit_length()
    rep = N_HEADS // N_KV_HEADS
    t0 = pl.program_id(1) * tq
    nkb = (t0 + tq + tk - 1) // tk
    qpos = t0 + lax.broadcasted_iota(jnp.int32, (tq, 1), 0)
    col = lax.broadcasted_iota(jnp.int32, (tq, tk), 1)
    nt = (((1,), (1,)), ((), ()))

    qi = qi_ref[0]
    wi = wi_ref[0]

    def score_block(kb, carry):
        start = pl.multiple_of(kb * tk, tk)
        sc = jnp.zeros((tq, tk), F32)
        for hd in range(IDX_HEADS):
            pair, side = divmod(hd, LANES // IDX_DIM)
            kib = ki_ref[0, pl.ds(start, tk), side * LANES:(side + 1) * LANES]
            lg = lax.dot_general(qi[:, pair * LANES:(pair + 1) * LANES], kib, nt,
                                 preferred_element_type=F32)
            sc = sc + jnp.maximum(lg, 0.0) * wi[:, hd:hd + 1]
        causal = (kb * tk + col) <= qpos
        key = jnp.where(causal, _ordered_key(sc), KEY_NINF)
        key_ref[:, pl.ds(pl.multiple_of(kb * tk, tk), tk)] = key
        return carry

    lax.fori_loop(0, nkb, score_block, 0)

    def count(pred):
        def body(kb, c):
            blk = key_ref[:, pl.ds(pl.multiple_of(kb * tk, tk), tk)]
            m = jnp.where(pred(blk, kb), 1.0, 0.0)
            for j in range(tk // LANES):
                c = c + m[:, j * LANES:(j + 1) * LANES]
            return c
        c = lax.fori_loop(0, nkb, body, jnp.zeros((tq, LANES), F32))
        return jnp.sum(c, axis=1, keepdims=True).astype(jnp.int32)

    def any_row(mask):
        return jnp.max(jnp.where(mask, 1.0, 0.0)) > 0.5

    lo0 = jnp.full((tq, 1), KEY_NINF, jnp.int32)
    hi0 = jnp.full((tq, 1), 0x7F800001, jnp.int32)
    done0 = jnp.zeros((tq, 1), jnp.int32)

    def bis_cond(st):
        lo, hi, done = st
        return any_row((done == 0) & (hi - lo != 1))

    def bis_body(st):
        lo, hi, done = st
        mid = (lo >> 1) + (hi >> 1) + (lo & hi & 1)
        c = count(lambda blk, kb: blk >= mid)
        active = (done == 0) & (hi - lo != 1)
        ge = c >= topk
        lo = jnp.where(active & ge, mid, lo)
        hi = jnp.where(active & jnp.logical_not(ge), mid, hi)
        done = jnp.where(active & (c == topk), 1, done)
        return lo, hi, done

    thr, _, done = lax.while_loop(bis_cond, bis_body, (lo0, hi0, done0))

    tied = (done == 0) & (thr > KEY_NINF)
    jcut0 = jnp.where(thr > KEY_NINF, jnp.int32(0x7FFFFFFF), jnp.int32(-1))

    def resolve_ties():
        need = topk - count(lambda blk, kb: blk > thr)

        def step(b, ans):
            cand = ans | (jnp.int32(1) << (pos_bits - 1 - b))
            c = count(lambda blk, kb: (blk == thr) & ((kb * tk + col) < cand))
            return jnp.where(c < need, cand, ans)

        ans = lax.fori_loop(0, pos_bits, step, jnp.zeros((tq, 1), jnp.int32))
        jcut_ref[...] = jnp.where(tied, ans, jcut0)

    jcut_ref[...] = jcut0
    pl.when(any_row(tied))(resolve_ties)
    jcut = jcut_ref[...]

    scale2 = HEAD_DIM ** -0.5 * math.log2(math.e)
    qg = [jnp.concatenate([q_ref[0, :, (g * rep + r) * HEAD_DIM:(g * rep + r + 1) * HEAD_DIM]
                           for r in range(rep)], axis=0) for g in range(N_KV_HEADS)]
    acc_ref[...] = jnp.zeros_like(acc_ref)
    m0 = jnp.full((rep * tq, 1), -jnp.inf, F32)
    l0 = jnp.zeros((rep * tq, 1), F32)

    def attn_block(kb, st):
        start = pl.multiple_of(kb * tk, tk)
        key = key_ref[:, pl.ds(start, tk)]
        sel = ((key > thr) | ((key == thr) & ((kb * tk + col) <= jcut))) & (key > KEY_NINF)
        sel = jnp.concatenate([sel] * rep, axis=0)
        new = []
        for g in range(N_KV_HEADS):
            m_old, l_old = st[2 * g], st[2 * g + 1]
            kblk = k_ref[0, pl.ds(start, tk), g * HEAD_DIM:(g + 1) * HEAD_DIM]
            vblk = v_ref[0, pl.ds(start, tk), g * HEAD_DIM:(g + 1) * HEAD_DIM]
            s = lax.dot_general(qg[g], kblk, nt, preferred_element_type=F32)
            s = jnp.where(sel, s, -jnp.inf)
            m_new = jnp.maximum(m_old, jnp.max(s, axis=1, keepdims=True))
            m_safe = jnp.where(m_new == -jnp.inf, 0.0, m_new)
            p = jnp.exp2((s - m_safe) * scale2)
            alpha = jnp.exp2((m_old - m_safe) * scale2)
            acc_ref[g] = alpha * acc_ref[g] + jnp.dot(p.astype(BF16), vblk,
                                                      preferred_element_type=F32)
            new += [m_new, alpha * l_old + jnp.sum(p, axis=1, keepdims=True)]
        return tuple(new)

    fin = lax.fori_loop(0, nkb, attn_block, (m0, l0) * N_KV_HEADS)
    for g in range(N_KV_HEADS):
        og = acc_ref[g] / fin[2 * g + 1]
        for r in range(rep):
            hd = g * rep + r
            o_ref[0, :, hd * HEAD_DIM:(hd + 1) * HEAD_DIM] = og[r * tq:(r + 1) * tq].astype(BF16)


def _dsa(q, k, v, qi, ki, wi, batch, seq):
    topk = min(TOPK_MAX, seq // 4)
    tq = DSA_TQ
    rep = N_HEADS // N_KV_HEADS
    r3 = lambda a: a.reshape(batch, seq, a.shape[-1])
    qrow = lambda width: pl.BlockSpec((1, tq, width), lambda b, i: (b, i, 0))
    full = lambda width: pl.BlockSpec((1, seq, width), lambda b, i: (b, 0, 0))
    out = pl.pallas_call(
        functools.partial(_dsa_kernel, topk=topk),
        out_shape=jax.ShapeDtypeStruct((batch, seq, N_HEADS * HEAD_DIM), BF16),
        grid=(batch, seq // tq),
        in_specs=[qrow(q.shape[-1]), qrow(qi.shape[-1]), qrow(wi.shape[-1]),
                  full(k.shape[-1]), full(v.shape[-1]), full(ki.shape[-1])],
        out_specs=qrow(N_HEADS * HEAD_DIM),
        scratch_shapes=[pltpu.VMEM((tq, seq), jnp.int32),
                        pltpu.VMEM((N_KV_HEADS, rep * tq, HEAD_DIM), F32),
                        pltpu.VMEM((tq, 1), jnp.int32)],
        compiler_params=_cparams("parallel", "arbitrary"),
        name="dsa_attention",
    )(r3(q), r3(qi), r3(wi), r3(k), r3(v), r3(ki))
    return out.reshape(batch * seq, N_HEADS * HEAD_DIM)


def _out_ln_kernel(h_ref, a_ref, w_ref, g_ref, b_ref, o_ref):
    mix = jnp.dot(a_ref[...], w_ref[...], preferred_element_type=F32)
    o_ref[...] = _layer_norm(DN_ALPHA * h_ref[...] + mix, g_ref[...], b_ref[...])


def _out_ln(h, a, w, g, b):
    m, d = h.shape
    tm = PROJ_ROWS
    return pl.pallas_call(
        _out_ln_kernel,
        out_shape=jax.ShapeDtypeStruct((m, d), F32),
        grid=(m // tm,),
        in_specs=[pl.BlockSpec((tm, d), lambda i: (i, 0)),
                  pl.BlockSpec((tm, a.shape[1]), lambda i: (i, 0)),
                  _const_spec(w.shape), _const_spec(g.shape), _const_spec(b.shape)],
        out_specs=pl.BlockSpec((tm, d), lambda i: (i, 0)),
        compiler_params=_cparams("parallel"),
        name="c_out_proj_ln",
    )(h, a, w, g, b)


def _chunk_cols(w, fc):
    *lead, d, n = w.shape
    w = w.reshape(*lead, d, n // fc, fc)
    return jnp.swapaxes(w, -3, -2).astype(BF16)


def kernel(x, p, positions, ln_g, ln_b, ffn_w1, ffn_w3, ffn_w2, ple_w_proj, ple_w_gate,
           ab_w_in, ab_w_out, conv_w, s5_lam_re, s5_lam_im, s5_log_dt, s5_b_re, s5_b_im,
           s5_c_re, s5_c_im, s5_d, s5_w_glu, s5_b_glu, c_w_in, c_w_out):
    batch, seq, d = x.shape
    m = batch * seq
    depth = ffn_w1.shape[0]
    d_ff = ffn_w1.shape[-1]

    w1c = _chunk_cols(ffn_w1, FF_CHUNK)
    w3c = _chunk_cols(ffn_w3, FF_CHUNK)
    w2c = ffn_w2.reshape(depth, 2, d_ff // FF_CHUNK, FF_CHUNK, d).astype(BF16)
    g3 = ln_g.reshape(depth, 3, 1, d)
    b3 = ln_b.reshape(depth, 3, 1, d)
    wp = ple_w_proj.astype(BF16)
    wg = ple_w_gate.astype(BF16)
    p2 = p.reshape(depth, m, p.shape[-1])

    pos = positions
    ca, sa = _rope_tables(pos, HEAD_DIM // ROT_FRAC, HEAD_DIM, 1)
    ci, si = _rope_tables(pos, IDX_DIM // ROT_FRAC, IDX_DIM, LANES // IDX_DIM)
    ca, sa, ci, si = (t.reshape(m, LANES) for t in (ca, sa, ci, si))
    c_in = c_w_in.shape[-1]
    c_pad = -c_in % LANES
    cw_in = jnp.pad(c_w_in, ((0, 0), (0, 0), (0, c_pad))).astype(BF16)

    h = x.reshape(m, d)
    for i in range(depth):
        h = _ffn_ln(h, w1c[i, 0], w3c[i, 0], w2c[i, 0], g3[i, 0], b3[i, 0])
        j = i // 2
        if i % 2 == 0:
            cg, u = _ab_proj(h, ab_w_in[j].astype(BF16))
            tables = _s5_tables(s5_lam_re[j], s5_lam_im[j], s5_log_dt[j], s5_b_re[j], s5_b_im[j],
                                s5_c_re[j], s5_c_im[j])
            y = _s5_linear(u, tables, batch, seq)
            h = _ab_mix(h, cg, u, y, conv_w[j], s5_d[j].reshape(1, -1), s5_w_glu[j].astype(BF16),
                        s5_b_glu[j].reshape(1, -1), ab_w_out[j].astype(BF16), g3[i, 1], b3[i, 1],
                        batch, seq)
        else:
            q, k, v, qi, ki, wi = _c_proj(h, cw_in[j], ca, sa, ci, si)
            a = _dsa(q, k, v, qi, ki, wi, batch, seq)
            h = _out_ln(h, a, c_w_out[j].astype(BF16), g3[i, 1], b3[i, 1])
        h = _ffn_ln(h, w1c[i, 1], w3c[i, 1], w2c[i, 1], g3[i, 2], b3[i, 2],
                    ple=(p2[i], wp[i], wg[i]))
    return h.reshape(batch, seq, d)
```

```python
import functools
import math

import jax
import jax.numpy as jnp
from jax import lax
from jax.experimental import pallas as pl
from jax.experimental.pallas import tpu as pltpu

F32 = jnp.float32
BF16 = jnp.bfloat16
HIGHEST = lax.Precision.HIGHEST

DEPTH = 4
LN_EPS = 1e-5
DN_ALPHA = (2 * DEPTH) ** 0.25
CONV_WIDTH = 3
CONV_CH = 512
S5_CH = 512
S5_GROUP = 16
S5_GROUPS = S5_CH // S5_GROUP
S5_STATE = 64
N_HEADS = 8
HEAD_DIM = 128
N_KV_HEADS = 2
IDX_HEADS = 8
IDX_DIM = 64
TOPK_MAX = 256
ROPE_THETA = 500000.0
ROT_FRAC = 4

LANES = 128
SUBLANES = 8
VMEM_LIMIT = 56 * 1024 * 1024

FFN_ROWS = 512
FF_CHUNK = 256
PROJ_ROWS = 512
S5_T = 16
DSA_TQ = 256
DSA_TQA = 128
DSA_TK = 512
COUNT_ROWS = 64
KEY_NINF = -0x7F800000


def _cparams(*sem):
    return pltpu.CompilerParams(dimension_semantics=sem, vmem_limit_bytes=VMEM_LIMIT)


def _layer_norm(y, g, b):
    mu = jnp.mean(y, axis=-1, keepdims=True)
    d = y - mu
    var = jnp.mean(d * d, axis=-1, keepdims=True)
    return d * lax.rsqrt(var + LN_EPS) * g + b


def _const_spec(shape):
    nd = len(shape)
    return pl.BlockSpec(shape, lambda *_: (0,) * nd, pipeline_mode=pl.Buffered(1))


def _ffn_body(x_ref, w1_ref, w3_ref, w2_ref, g_ref, b_ref, acc_ref):
    x = x_ref[...]
    xb = x.astype(BF16)
    acc_ref[...] = jnp.zeros_like(acc_ref)

    def step(c, carry):
        a = jnp.dot(xb, w1_ref[c], preferred_element_type=F32)
        gt = jnp.dot(xb, w3_ref[c], preferred_element_type=F32)
        act = (a * jax.nn.sigmoid(a)) * gt
        acc_ref[...] += jnp.dot(act.astype(BF16), w2_ref[c], preferred_element_type=F32)
        return carry

    lax.fori_loop(0, w1_ref.shape[0], step, 0, unroll=True)
    y = DN_ALPHA * x + 0.5 * acc_ref[...]
    return _layer_norm(y, g_ref[...], b_ref[...])


def _ffn_kernel(x_ref, w1_ref, w3_ref, w2_ref, g_ref, b_ref, o_ref, acc_ref):
    o_ref[...] = _ffn_body(x_ref, w1_ref, w3_ref, w2_ref, g_ref, b_ref, acc_ref)


def _ffn_ple_kernel(x_ref, w1_ref, w3_ref, w2_ref, g_ref, b_ref, p_ref, wp_ref, wg_ref,
                    o_ref, acc_ref):
    h = _ffn_body(x_ref, w1_ref, w3_ref, w2_ref, g_ref, b_ref, acc_ref)
    gate = jax.nn.sigmoid(jnp.dot(h.astype(BF16), wg_ref[...], preferred_element_type=F32))
    pe = jnp.dot(p_ref[...].astype(BF16), wp_ref[...], preferred_element_type=F32)
    o_ref[...] = h + pe * gate


def _ffn_ln(x, w1c, w3c, w2c, g, b, ple=None):
    m, d = x.shape
    nc, _, fc = w1c.shape
    row = pl.BlockSpec((FFN_ROWS, d), lambda i: (i, 0))
    in_specs = [row, _const_spec((nc, d, fc)), _const_spec((nc, d, fc)), _const_spec((nc, fc, d)),
                _const_spec((1, d)), _const_spec((1, d))]
    args = [x, w1c, w3c, w2c, g, b]
    body = _ffn_kernel
    name = "ffn_ln"
    if ple is not None:
        p, wp, wg = ple
        in_specs += [pl.BlockSpec((FFN_ROWS, p.shape[1]), lambda i: (i, 0)),
                     _const_spec(wp.shape), _const_spec(wg.shape)]
        args += [p, wp, wg]
        body = _ffn_ple_kernel
        name = "ffn_ln_ple"
    return pl.pallas_call(
        body,
        out_shape=jax.ShapeDtypeStruct((m, d), F32),
        grid=(m // FFN_ROWS,),
        in_specs=in_specs,
        out_specs=row,
        scratch_shapes=[pltpu.VMEM((FFN_ROWS, d), F32)],
        compiler_params=_cparams("parallel"),
        name=name,
    )(*args)


def _ab_proj_kernel(x_ref, w_ref, cg_ref, u_ref):
    y = jnp.dot(x_ref[...].astype(BF16), w_ref[...], preferred_element_type=F32)
    n_cg = cg_ref.shape[1]
    cg_ref[...] = y[:, :n_cg]
    u_ref[...] = y[:, n_cg:]


def _ab_proj(x, w):
    m, d = x.shape
    n = w.shape[1]
    n_cg = 3 * CONV_CH
    return pl.pallas_call(
        _ab_proj_kernel,
        out_shape=(jax.ShapeDtypeStruct((m, n_cg), F32), jax.ShapeDtypeStruct((m, n - n_cg), F32)),
        grid=(m // PROJ_ROWS,),
        in_specs=[pl.BlockSpec((PROJ_ROWS, d), lambda i: (i, 0)), _const_spec((d, n))],
        out_specs=(pl.BlockSpec((PROJ_ROWS, n_cg), lambda i: (i, 0)),
                   pl.BlockSpec((PROJ_ROWS, n - n_cg), lambda i: (i, 0))),
        compiler_params=_cparams("parallel"),
        name="ab_in_proj",
    )(x, w)


def _s5_tables(lam_re, lam_im, log_dt, b_re, b_im, c_re, c_im):
    lr = jnp.minimum(lam_re, -1e-4)
    li = lam_im
    dt = jnp.exp(log_dt)[:, None]
    mag = jnp.exp(lr * dt)
    ab_re = mag * jnp.cos(li * dt)
    ab_im = mag * jnp.sin(li * dt)
    nr, ni = ab_re - 1.0, ab_im
    den = lr * lr + li * li
    f_re = (nr * lr + ni * li) / den
    f_im = (ni * lr - nr * li) / den
    bb_re = f_re[..., None] * b_re - f_im[..., None] * b_im
    bb_im = f_re[..., None] * b_im + f_im[..., None] * b_re
    pw_re = [jnp.ones_like(ab_re)]
    pw_im = [jnp.zeros_like(ab_im)]
    for _ in range(S5_T):
        pr, pi = pw_re[-1], pw_im[-1]
        pw_re.append(pr * ab_re - pi * ab_im)
        pw_im.append(pr * ab_im + pi * ab_re)
    pw_re = jnp.stack(pw_re)
    pw_im = jnp.stack(pw_im)
    cp_re = c_re[None] * pw_re[:, :, None, :] - c_im[None] * pw_im[:, :, None, :]
    cp_im = c_re[None] * pw_im[:, :, None, :] + c_im[None] * pw_re[:, :, None, :]
    taps = (jnp.einsum('ngcp,gpd->ngcd', cp_re[:S5_T], bb_re, precision=HIGHEST)
            - jnp.einsum('ngcp,gpd->ngcd', cp_im[:S5_T], bb_im, precision=HIGHEST))
    s_idx = jnp.arange(S5_T)[:, None]
    t_idx = jnp.arange(S5_T)[None, :]
    lag = t_idx - s_idx
    k_st = taps[jnp.clip(lag, 0, S5_T - 1)]
    k_st = jnp.where((lag >= 0)[:, :, None, None, None], k_st, 0.0)
    toep = jnp.transpose(k_st, (2, 0, 4, 1, 3)).reshape(S5_GROUPS, S5_T * S5_GROUP, S5_T * S5_GROUP)
    rev_re = pw_re[S5_T - 1::-1][:S5_T]
    rev_im = pw_im[S5_T - 1::-1][:S5_T]
    win_re = rev_re[..., None] * bb_re[None] - rev_im[..., None] * bb_im[None]
    win_im = rev_re[..., None] * bb_im[None] + rev_im[..., None] * bb_re[None]
    w_in = jnp.concatenate([win_re, win_im], axis=2)
    w_in = jnp.transpose(w_in, (1, 0, 3, 2)).reshape(S5_GROUPS, S5_T * S5_GROUP, 2 * S5_STATE)
    wo_re = jnp.transpose(cp_re[1:], (1, 3, 0, 2))
    wo_im = jnp.transpose(-cp_im[1:], (1, 3, 0, 2))
    w_out = jnp.concatenate([wo_re, wo_im], axis=1).reshape(S5_GROUPS, 2 * S5_STATE, S5_T * S5_GROUP)
    a1 = jnp.concatenate([pw_re[S5_T], pw_re[S5_T]], axis=-1)
    a2 = jnp.concatenate([-pw_im[S5_T], pw_im[S5_T]], axis=-1)
    return toep, w_in, w_out, a1, a2


def _s5_local_kernel(uc_ref, win_ref, o_ref):
    o_ref[0] = jnp.dot(uc_ref[0], win_ref[0], preferred_element_type=F32, precision=HIGHEST)


def _s5_local(uc, w_in):
    g, rows, width = uc.shape
    ns = w_in.shape[2]
    return pl.pallas_call(
        _s5_local_kernel,
        out_shape=jax.ShapeDtypeStruct((g, rows, ns), F32),
        grid=(g,),
        in_specs=[pl.BlockSpec((1, rows, width), lambda i: (i, 0, 0)),
                  pl.BlockSpec((1, width, ns), lambda i: (i, 0, 0))],
        out_specs=pl.BlockSpec((1, rows, ns), lambda i: (i, 0, 0)),
        compiler_params=_cparams("parallel"),
        name="s5_chunk_state_in",
    )(uc, w_in)


def _s5_scan_kernel(g_ref, a1_ref, a2_ref, o_ref):
    a1 = a1_ref[...]
    a2 = a2_ref[...]
    half = a1.shape[1] // 2

    def step(j, x):
        o_ref[j] = x
        return a1 * x + a2 * pltpu.roll(x, half, 1) + g_ref[j]

    lax.fori_loop(0, g_ref.shape[0], step, jnp.zeros(a1.shape, F32))


def _s5_scan(gl, a1, a2):
    nch, rows, ns = gl.shape
    rb = 32
    return pl.pallas_call(
        _s5_scan_kernel,
        out_shape=jax.ShapeDtypeStruct((nch, rows, ns), F32),
        grid=(rows // rb,),
        in_specs=[pl.BlockSpec((nch, rb, ns), lambda i: (0, i, 0)),
                  pl.BlockSpec((rb, ns), lambda i: (i, 0)),
                  pl.BlockSpec((rb, ns), lambda i: (i, 0))],
        out_specs=pl.BlockSpec((nch, rb, ns), lambda i: (0, i, 0)),
        compiler_params=_cparams("parallel"),
        name="s5_chunk_scan",
    )(gl, a1, a2)


def _s5_out_kernel(uc_ref, xp_ref, toep_ref, wout_ref, o_ref):
    y = jnp.dot(uc_ref[0], toep_ref[0], preferred_element_type=F32, precision=HIGHEST)
    y = y + jnp.dot(xp_ref[0], wout_ref[0], preferred_element_type=F32, precision=HIGHEST)
    o_ref[0] = y


def _s5_out(uc, xprev, toep, w_out):
    g, rows, width = uc.shape
    ns = xprev.shape[2]
    return pl.pallas_call(
        _s5_out_kernel,
        out_shape=jax.ShapeDtypeStruct((g, rows, width), F32),
        grid=(g,),
        in_specs=[pl.BlockSpec((1, rows, width), lambda i: (i, 0, 0)),
                  pl.BlockSpec((1, rows, ns), lambda i: (i, 0, 0)),
                  pl.BlockSpec((1, width, width), lambda i: (i, 0, 0)),
                  pl.BlockSpec((1, ns, width), lambda i: (i, 0, 0))],
        out_specs=pl.BlockSpec((1, rows, width), lambda i: (i, 0, 0)),
        compiler_params=_cparams("parallel"),
        name="s5_chunk_out",
    )(uc, xprev, toep, w_out)


def _s5_linear(u, tables, batch, seq):
    toep, w_in, w_out, a1, a2 = tables
    nch = seq // S5_T
    uc = u.reshape(batch, nch, S5_T, S5_GROUPS, S5_GROUP)
    uc = jnp.transpose(uc, (3, 0, 1, 2, 4)).reshape(S5_GROUPS, batch * nch, S5_T * S5_GROUP)
    gl = _s5_local(uc, w_in)
    ns = gl.shape[2]
    gl = jnp.transpose(gl.reshape(S5_GROUPS, batch, nch, ns), (2, 1, 0, 3))
    gl = gl.reshape(nch, batch * S5_GROUPS, ns)
    xprev = _s5_scan(gl, jnp.tile(a1, (batch, 1)), jnp.tile(a2, (batch, 1)))
    xprev = jnp.transpose(xprev.reshape(nch, batch, S5_GROUPS, ns), (2, 1, 0, 3))
    xprev = xprev.reshape(S5_GROUPS, batch * nch, ns)
    yc = _s5_out(uc, xprev, toep, w_out)
    yc = yc.reshape(S5_GROUPS, batch, nch, S5_T, S5_GROUP)
    return jnp.transpose(yc, (1, 2, 3, 0, 4)).reshape(batch * seq, S5_CH)


def _gelu_tanh(x):
    c = math.sqrt(2.0 / math.pi)
    return 0.5 * x * (1.0 + jnp.tanh(c * (x + 0.044715 * (x * x * x))))


def _ab_mix_kernel(h_ref, cg_ref, u_ref, y_ref, cw_ref, d_ref, wglu_ref, bglu_ref, wout_ref,
                   g_ref, b_ref, o_ref, uext_ref):
    tm = h_ref.shape[1]
    cc = CONV_CH
    halo = SUBLANES

    @pl.when(pl.program_id(1) == 0)
    def _():
        uext_ref[0:halo, :] = jnp.zeros((halo, cc), F32)

    cg = cg_ref[0]
    uc = cg[:, 2 * cc:3 * cc] * cg[:, 0:cc]
    uext_ref[halo:halo + tm, :] = uc
    v = (cw_ref[0:1, :] * uext_ref[halo - 2:halo - 2 + tm, :]
         + cw_ref[1:2, :] * uext_ref[halo - 1:halo - 1 + tm, :]
         + cw_ref[2:3, :] * uc)
    ya = cg[:, cc:2 * cc] * v
    uext_ref[0:halo, :] = uc[tm - halo:tm, :]

    y2 = y_ref[0] + d_ref[...] * u_ref[0]
    z = _gelu_tanh(y2)
    gate = jax.nn.sigmoid(jnp.dot(z.astype(BF16), wglu_ref[...], preferred_element_type=F32)
                          + bglu_ref[...])
    yb = z * gate
    mix = (jnp.dot(ya.astype(BF16), wout_ref[0:cc, :], preferred_element_type=F32)
           + jnp.dot(yb.astype(BF16), wout_ref[cc:, :], preferred_element_type=F32))
    o_ref[0] = _layer_norm(DN_ALPHA * h_ref[0] + mix, g_ref[...], b_ref[...])


def _ab_mix(h, cg, u, y, conv_w, d_skip, w_glu, b_glu, w_out, g, b, batch, seq):
    d = h.shape[-1]
    tm = PROJ_ROWS

    def row(width):
        return pl.BlockSpec((1, tm, width), lambda bi, si: (bi, si, 0))

    r3 = lambda a: a.reshape(batch, seq, a.shape[-1])
    out = pl.pallas_call(
        _ab_mix_kernel,
        out_shape=jax.ShapeDtypeStruct((batch, seq, d), F32),
        grid=(batch, seq // tm),
        in_specs=[row(d), row(3 * CONV_CH), row(S5_CH), row(S5_CH),
                  _const_spec(conv_w.shape), _const_spec(d_skip.shape), _const_spec(w_glu.shape),
                  _const_spec(b_glu.shape), _const_spec(w_out.shape),
                  _const_spec(g.shape), _const_spec(b.shape)],
        out_specs=row(d),
        scratch_shapes=[pltpu.VMEM((tm + SUBLANES, CONV_CH), F32)],
        compiler_params=_cparams("parallel", "arbitrary"),
        name="ab_mix_out_ln",
    )(r3(h), r3(cg), r3(u), r3(y), conv_w, d_skip, w_glu, b_glu, w_out, g, b)
    return out.reshape(batch * seq, d)


def _rope_tables(positions, rot_dim, width, repeat):
    half = rot_dim // 2
    inv = ROPE_THETA ** (-jnp.arange(0, rot_dim, 2, dtype=F32) / rot_dim)
    ang = positions.astype(F32)[..., None] * inv
    cos, sin = jnp.cos(ang), jnp.sin(ang)
    lead = cos.shape[:-1]
    ones = jnp.ones(lead + (width - 2 * half,), F32)
    zeros = jnp.zeros(lead + (width - 2 * half,), F32)
    ct = jnp.concatenate([cos, cos, ones], axis=-1)
    st = jnp.concatenate([-sin, sin, zeros], axis=-1)
    return jnp.tile(ct, (1, 1, repeat)), jnp.tile(st, (1, 1, repeat))


def _rotate(t, cos_t, sin_t, half, period):
    lane = lax.broadcasted_iota(jnp.int32, t.shape, 1)
    first = (lane % period) < half
    partner = jnp.where(first, pltpu.roll(t, LANES - half, 1), pltpu.roll(t, half, 1))
    return t * cos_t + partner * sin_t


def _c_proj_kernel(x_ref, w_ref, ca_ref, sa_ref, ci_ref, si_ref,
                   q_ref, k_ref, vt_ref, qi_ref, ki_ref, wit_ref):
    y = jnp.dot(x_ref[...].astype(BF16), w_ref[...], preferred_element_type=F32)
    ca, sa, ci, si = ca_ref[...], sa_ref[...], ci_ref[...], si_ref[...]
    half_a = HEAD_DIM // ROT_FRAC // 2
    half_i = IDX_DIM // ROT_FRAC // 2
    nq = N_HEADS * HEAD_DIM
    nkv = N_KV_HEADS * HEAD_DIM
    nqi = IDX_HEADS * IDX_DIM
    for hd in range(N_HEADS):
        t = y[:, hd * LANES:(hd + 1) * LANES]
        q_ref[:, hd * LANES:(hd + 1) * LANES] = _rotate(t, ca, sa, half_a, HEAD_DIM).astype(BF16)
    for hd in range(N_KV_HEADS):
        t = y[:, nq + hd * LANES:nq + (hd + 1) * LANES]
        k_ref[:, hd * LANES:(hd + 1) * LANES] = _rotate(t, ca, sa, half_a, HEAD_DIM).astype(BF16)
    vt_ref[0] = y[:, nq + nkv:nq + 2 * nkv].T.astype(BF16)
    o = nq + 2 * nkv
    for pr in range(nqi // LANES):
        t = y[:, o + pr * LANES:o + (pr + 1) * LANES]
        qi_ref[:, pr * LANES:(pr + 1) * LANES] = _rotate(t, ci, si, half_i, IDX_DIM).astype(BF16)
    o += nqi
    t = y[:, o:o + LANES]
    lane = lax.broadcasted_iota(jnp.int32, t.shape, 1)
    ki_lo = jnp.where(lane < IDX_DIM, _rotate(t, ci, si, half_i, IDX_DIM), 0.0)
    ki_ref[:, 0:LANES] = ki_lo.astype(BF16)
    ki_ref[:, LANES:2 * LANES] = pltpu.roll(ki_lo, IDX_DIM, 1).astype(BF16)
    wit = t.T[IDX_DIM:IDX_DIM + IDX_HEADS, :]
    wit_ref[0] = wit * (IDX_HEADS ** -0.5 * IDX_DIM ** -0.5)


def _c_proj(x, w, ca, sa, ci, si, batch, seq):
    m, d = x.shape
    n = w.shape[1]
    tm = PROJ_ROWS
    spb = seq // tm
    nq = N_HEADS * HEAD_DIM
    nkv = N_KV_HEADS * HEAD_DIM
    nqi = IDX_HEADS * IDX_DIM
    row = lambda width: pl.BlockSpec((tm, width), lambda i: (i, 0))
    col = lambda height: pl.BlockSpec((1, height, tm), lambda i: (i // spb, 0, i % spb))
    return pl.pallas_call(
        _c_proj_kernel,
        out_shape=(jax.ShapeDtypeStruct((m, nq), BF16), jax.ShapeDtypeStruct((m, nkv), BF16),
                   jax.ShapeDtypeStruct((batch, nkv, seq), BF16), jax.ShapeDtypeStruct((m, nqi), BF16),
                   jax.ShapeDtypeStruct((m, 2 * LANES), BF16),
                   jax.ShapeDtypeStruct((batch, IDX_HEADS, seq), F32)),
        grid=(m // tm,),
        in_specs=[row(d), _const_spec((d, n)), row(LANES), row(LANES), row(LANES), row(LANES)],
        out_specs=(row(nq), row(nkv), col(nkv), row(nqi), row(2 * LANES), col(IDX_HEADS)),
        compiler_params=_cparams("parallel"),
        name="c_in_proj_rope",
    )(x, w, ca, sa, ci, si)


def _ordered_key(s):
    bits = pltpu.bitcast(s, jnp.int32)
    return jnp.where(bits < 0, -(bits ^ jnp.int32(-0x80000000)), bits)


def _unkey(k):
    return pltpu.bitcast(jnp.where(k < 0, (-k) ^ jnp.int32(-0x80000000), k), F32)


def _dsa_kernel(q_ref, qi_ref, wit_ref, k_ref, vt_ref, ki_ref, o_ref,
                key_ref, acc_ref, jcut_ref, sa_ref, sb_ref, l_ref, *, topk):
    tq, tk, tqa = DSA_TQ, DSA_TK, DSA_TQA
    pos_bits = (k_ref.shape[1] - 1).bit_length()
    rep = N_HEADS // N_KV_HEADS
    t0 = pl.program_id(1) * tq
    nkb = (t0 + tq + tk - 1) // tk
    qpos = t0 + lax.broadcasted_iota(jnp.int32, (1, tq), 1)
    krow = lax.broadcasted_iota(jnp.int32, (tk, 1), 0)
    nt = (((1,), (1,)), ((), ()))

    qi = qi_ref[0]
    wit = wit_ref[0]

    def score_block(kb, carry):
        smax, smin = carry
        start = pl.multiple_of(kb * tk, tk)
        sc = jnp.zeros((tk, tq), F32)
        for hd in range(IDX_HEADS):
            pair, side = divmod(hd, LANES // IDX_DIM)
            kib = ki_ref[0, pl.ds(start, tk), side * LANES:(side + 1) * LANES]
            lg = lax.dot_general(kib, qi[:, pair * LANES:(pair + 1) * LANES], nt,
                                 preferred_element_type=F32)
            sc = sc + jnp.maximum(lg, 0.0) * wit[hd:hd + 1, :]
        causal = (start + krow) <= qpos
        key_ref[pl.ds(start, tk), :] = jnp.where(causal, _ordered_key(sc), KEY_NINF)
        smax = jnp.maximum(smax, jnp.max(jnp.where(causal, sc, -jnp.inf), axis=0, keepdims=True))
        smin = jnp.minimum(smin, jnp.min(jnp.where(causal, sc, jnp.inf), axis=0, keepdims=True))
        return smax, smin

    smax, smin = lax.fori_loop(0, nkb, score_block,
                               (jnp.full((1, tq), -jnp.inf, F32), jnp.full((1, tq), jnp.inf, F32)))

    def count(pred):
        def body(kb, c):
            start = pl.multiple_of(kb * tk, tk)
            m = jnp.where(pred(key_ref[pl.ds(start, tk), :], start), 1.0, 0.0)
            return c + jnp.sum(m.reshape(tk // COUNT_ROWS, COUNT_ROWS, tq), axis=0)
        c = lax.fori_loop(0, nkb, body, jnp.zeros((COUNT_ROWS, tq), F32))
        return jnp.sum(c, axis=0, keepdims=True).astype(jnp.int32)

    def any_query(mask):
        return jnp.max(jnp.where(mask, 1.0, 0.0)) > 0.5

    few = (qpos + 1) <= topk
    lo0 = jnp.where(few, KEY_NINF, _ordered_key(smin))
    hi0 = _ordered_key(smax) + 1
    done0 = jnp.where(few, 1, 0)

    def bisect(st, by_value):
        lo, hi, done = st
        if by_value:
            mid = _ordered_key(0.5 * _unkey(lo) + 0.5 * _unkey(jnp.minimum(hi, -KEY_NINF)))
            mid = jnp.minimum(jnp.maximum(mid, lo + 1), hi - 1)
        else:
            mid = (lo >> 1) + (hi >> 1) + (lo & hi & 1)
        c = count(lambda blk, start: blk >= mid)
        active = (done == 0) & (hi - lo != 1)
        ge = c >= topk
        lo = jnp.where(active & ge, mid, lo)
        hi = jnp.where(active & jnp.logical_not(ge), mid, hi)
        done = jnp.where(active & (c == topk), 1, done)
        return lo, hi, done

    def bis_cond(st):
        lo, hi, done = st
        return any_query((done == 0) & (hi - lo != 1))

    def bis_body(st):
        return bisect(bisect(bisect(st, True), True), False)

    thr, _, done = lax.while_loop(bis_cond, bis_body, (lo0, hi0, done0))

    tied = (done == 0) & (thr > KEY_NINF)
    jcut0 = jnp.where(thr > KEY_NINF, jnp.int32(0x7FFFFFFF), jnp.int32(-1))

    def resolve_ties():
        need = topk - count(lambda blk, start: blk > thr)

        def step(b, ans):
            cand = ans | (jnp.int32(1) << (pos_bits - 1 - b))
            c = count(lambda blk, start: (blk == thr) & ((start + krow) < cand))
            return jnp.where(c < need, cand, ans)

        ans = lax.fori_loop(0, pos_bits, step, jnp.zeros((1, tq), jnp.int32))
        jcut_ref[...] = jnp.where(tied, ans, jcut0)

    jcut_ref[...] = jcut0
    pl.when(any_query(tied))(resolve_ties)
    jcut = jcut_ref[...]

    scale2 = HEAD_DIM ** -0.5 * math.log2(math.e)
    for u in range(tq // tqa):
        qs = slice(u * tqa, (u + 1) * tqa)
        thr_u, jcut_u = thr[:, qs], jcut[:, qs]
        nkb_u = (t0 + (u + 1) * tqa + tk - 1) // tk
        qg = [jnp.concatenate([q_ref[0, qs, (g * rep + r) * HEAD_DIM:(g * rep + r + 1) * HEAD_DIM]
                               for r in range(rep)], axis=0) for g in range(N_KV_HEADS)]
        acc_ref[...] = jnp.zeros_like(acc_ref)
        ncol = rep * tqa

        def col_reduce(op, x):
            part = op(x.reshape(tk // COUNT_ROWS, COUNT_ROWS, ncol), axis=0)
            return op(part, axis=0, keepdims=True)

        def score_into(s_ref, kb):
            start = pl.multiple_of(kb * tk, tk)
            key = key_ref[pl.ds(start, tk), qs]
            sel = (((key > thr_u) | ((key == thr_u) & ((start + krow) <= jcut_u)))
                   & (key > KEY_NINF))
            sel = jnp.concatenate([sel] * rep, axis=1)
            cmax = []
            for g in range(N_KV_HEADS):
                kblk = k_ref[0, pl.ds(start, tk), g * HEAD_DIM:(g + 1) * HEAD_DIM]
                s = lax.dot_general(kblk, qg[g], nt, preferred_element_type=F32)
                s = jnp.where(sel, s, -jnp.inf)
                s_ref[g] = s
                cmax.append(col_reduce(jnp.max, s))
            return cmax

        def accumulate(s_ref, kb, g, m_prev, m_cur, l_old):
            start = pl.multiple_of(kb * tk, tk)
            vtb = vt_ref[0, g * HEAD_DIM:(g + 1) * HEAD_DIM, pl.ds(start, tk)]
            m_safe = jnp.where(m_cur == -jnp.inf, 0.0, m_cur)
            p = jnp.exp2((s_ref[g] - m_safe) * scale2)
            alpha = jnp.exp2((m_prev - m_safe) * scale2)
            acc_ref[g] = alpha * acc_ref[g] + jnp.dot(vtb, p.astype(BF16),
                                                      preferred_element_type=F32)
            return alpha * l_old + col_reduce(jnp.sum, p)

        def step(s_cur, s_next, kb, st):
            cmax = score_into(s_next, kb + 1)
            new = []
            for g in range(N_KV_HEADS):
                m_prev, m_cur, l_old = st[3 * g:3 * g + 3]
                l_new = accumulate(s_cur, kb, g, m_prev, m_cur, l_old)
                new += [m_cur, jnp.maximum(m_cur, cmax[g]), l_new]
            return tuple(new)

        def finish(s_cur, kb, st):
            for g in range(N_KV_HEADS):
                m_prev, m_cur, l_old = st[3 * g:3 * g + 3]
                l_ref[g] = accumulate(s_cur, kb, g, m_prev, m_cur, l_old)

        cmax0 = score_into(sa_ref, 0)
        st0 = ()
        for g in range(N_KV_HEADS):
            st0 += (jnp.full((1, ncol), -jnp.inf, F32), cmax0[g], jnp.zeros((1, ncol), F32))

        def pair(j, st):
            return step(sb_ref, sa_ref, 2 * j + 1, step(sa_ref, sb_ref, 2 * j, st))

        st = lax.fori_loop(0, (nkb_u - 1) // 2, pair, st0)
        last = nkb_u - 1

        @pl.when(last % 2 == 0)
        def _():
            finish(sa_ref, last, st)

        @pl.when(last % 2 == 1)
        def _():
            finish(sb_ref, last, step(sa_ref, sb_ref, last - 1, st))

        for g in range(N_KV_HEADS):
            og = acc_ref[g] / l_ref[g]
            for r in range(rep):
                hd = g * rep + r
                o_ref[0, qs, hd * HEAD_DIM:(hd + 1) * HEAD_DIM] = (
                    og[:, r * tqa:(r + 1) * tqa].T.astype(BF16))


def _dsa(q, k, vt, qi, ki, wit, batch, seq):
    topk = min(TOPK_MAX, seq // 4)
    tq = DSA_TQ
    rep = N_HEADS // N_KV_HEADS
    r3 = lambda a: a.reshape(batch, seq, a.shape[-1])
    qrow = lambda width: pl.BlockSpec((1, tq, width), lambda b, i: (b, i, 0))
    full = lambda width: pl.BlockSpec((1, seq, width), lambda b, i: (b, 0, 0))
    out = pl.pallas_call(
        functools.partial(_dsa_kernel, topk=topk),
        out_shape=jax.ShapeDtypeStruct((batch, seq, N_HEADS * HEAD_DIM), BF16),
        grid=(batch, seq // tq),
        in_specs=[qrow(q.shape[-1]), qrow(qi.shape[-1]),
                  pl.BlockSpec((1, wit.shape[1], tq), lambda b, i: (b, 0, i)),
                  full(k.shape[-1]),
                  pl.BlockSpec((1, vt.shape[1], seq), lambda b, i: (b, 0, 0)),
                  full(ki.shape[-1])],
        out_specs=qrow(N_HEADS * HEAD_DIM),
        scratch_shapes=[pltpu.VMEM((seq, tq), jnp.int32),
                        pltpu.VMEM((N_KV_HEADS, HEAD_DIM, rep * DSA_TQA), F32),
                        pltpu.VMEM((1, tq), jnp.int32),
                        pltpu.VMEM((N_KV_HEADS, DSA_TK, rep * DSA_TQA), F32),
                        pltpu.VMEM((N_KV_HEADS, DSA_TK, rep * DSA_TQA), F32),
                        pltpu.VMEM((N_KV_HEADS, 1, rep * DSA_TQA), F32)],
        compiler_params=_cparams("parallel", "arbitrary"),
        name="dsa_attention",
    )(r3(q), r3(qi), wit, r3(k), vt, r3(ki))
    return out.reshape(batch * seq, N_HEADS * HEAD_DIM)


def _out_ln_kernel(h_ref, a_ref, w_ref, g_ref, b_ref, o_ref):
    mix = jnp.dot(a_ref[...], w_ref[...], preferred_element_type=F32)
    o_ref[...] = _layer_norm(DN_ALPHA * h_ref[...] + mix, g_ref[...], b_ref[...])


def _out_ln(h, a, w, g, b):
    m, d = h.shape
    tm = PROJ_ROWS
    return pl.pallas_call(
        _out_ln_kernel,
        out_shape=jax.ShapeDtypeStruct((m, d), F32),
        grid=(m // tm,),
        in_specs=[pl.BlockSpec((tm, d), lambda i: (i, 0)),
                  pl.BlockSpec((tm, a.shape[1]), lambda i: (i, 0)),
                  _const_spec(w.shape), _const_spec(g.shape), _const_spec(b.shape)],
        out_specs=pl.BlockSpec((tm, d), lambda i: (i, 0)),
        compiler_params=_cparams("parallel"),
        name="c_out_proj_ln",
    )(h, a, w, g, b)


def _chunk_cols(w, fc):
    *lead, d, n = w.shape
    w = w.reshape(*lead, d, n // fc, fc)
    return jnp.swapaxes(w, -3, -2).astype(BF16)


def kernel(x, p, positions, ln_g, ln_b, ffn_w1, ffn_w3, ffn_w2, ple_w_proj, ple_w_gate,
           ab_w_in, ab_w_out, conv_w, s5_lam_re, s5_lam_im, s5_log_dt, s5_b_re, s5_b_im,
           s5_c_re, s5_c_im, s5_d, s5_w_glu, s5_b_glu, c_w_in, c_w_out):
    batch, seq, d = x.shape
    m = batch * seq
    depth = ffn_w1.shape[0]
    d_ff = ffn_w1.shape[-1]

    w1c = _chunk_cols(ffn_w1, FF_CHUNK)
    w3c = _chunk_cols(ffn_w3, FF_CHUNK)
    w2c = ffn_w2.reshape(depth, 2, d_ff // FF_CHUNK, FF_CHUNK, d).astype(BF16)
    g3 = ln_g.reshape(depth, 3, 1, d)
    b3 = ln_b.reshape(depth, 3, 1, d)
    wp = ple_w_proj.astype(BF16)
    wg = ple_w_gate.astype(BF16)
    p2 = p.reshape(depth, m, p.shape[-1])

    pos = positions
    ca, sa = _rope_tables(pos, HEAD_DIM // ROT_FRAC, HEAD_DIM, 1)
    ci, si = _rope_tables(pos, IDX_DIM // ROT_FRAC, IDX_DIM, LANES // IDX_DIM)
    ca, sa, ci, si = (t.reshape(m, LANES) for t in (ca, sa, ci, si))
    c_in = c_w_in.shape[-1]
    c_pad = -c_in % LANES
    cw_in = jnp.pad(c_w_in, ((0, 0), (0, 0), (0, c_pad))).astype(BF16)

    h = x.reshape(m, d)
    for i in range(depth):
        h = _ffn_ln(h, w1c[i, 0], w3c[i, 0], w2c[i, 0], g3[i, 0], b3[i, 0])
        j = i // 2
        if i % 2 == 0:
            cg, u = _ab_proj(h, ab_w_in[j].astype(BF16))
            tables = _s5_tables(s5_lam_re[j], s5_lam_im[j], s5_log_dt[j], s5_b_re[j], s5_b_im[j],
                                s5_c_re[j], s5_c_im[j])
            y = _s5_linear(u, tables, batch, seq)
            h = _ab_mix(h, cg, u, y, conv_w[j], s5_d[j].reshape(1, -1), s5_w_glu[j].astype(BF16),
                        s5_b_glu[j].reshape(1, -1), ab_w_out[j].astype(BF16), g3[i, 1], b3[i, 1],
                        batch, seq)
        else:
            q, k, vt, qi, ki, wit = _c_proj(h, cw_in[j], ca, sa, ci, si, batch, seq)
            a = _dsa(q, k, vt, qi, ki, wit, batch, seq)
            h = _out_ln(h, a, c_w_out[j].astype(BF16), g3[i, 1], b3[i, 1])
        h = _ffn_ln(h, w1c[i, 1], w3c[i, 1], w2c[i, 1], g3[i, 2], b3[i, 2],
                    ple=(p2[i], wp[i], wg[i]))
    return h.reshape(batch, seq, d)
```

```python
import functools
import math

import jax
import jax.numpy as jnp
from jax import lax
from jax.experimental import pallas as pl
from jax.experimental.pallas import tpu as pltpu

F32 = jnp.float32
BF16 = jnp.bfloat16
HIGHEST = lax.Precision.HIGHEST

DEPTH = 4
LN_EPS = 1e-5
DN_ALPHA = (2 * DEPTH) ** 0.25
CONV_WIDTH = 3
CONV_CH = 512
S5_CH = 512
S5_GROUP = 16
S5_GROUPS = S5_CH // S5_GROUP
S5_STATE = 64
N_HEADS = 8
HEAD_DIM = 128
N_KV_HEADS = 2
IDX_HEADS = 8
IDX_DIM = 64
TOPK_MAX = 256
ROPE_THETA = 500000.0
ROT_FRAC = 4

LANES = 128
SUBLANES = 8
VMEM_LIMIT = 60 * 1024 * 1024

FFN_ROWS = 512
FF_CHUNK = 256
PROJ_ROWS = 512
S5_T = 16
DSA_TQ = 256
DSA_TQA = 128
DSA_TK = 512
COUNT_ROWS = 64
BF16_ROWS = 16
VT_ROWS = HEAD_DIM + BF16_ROWS
QK_SCALE_LOG2 = HEAD_DIM ** -0.5 * math.log2(math.e)
KEY_NINF = -0x7F800000


def _cparams(*sem):
    return pltpu.CompilerParams(dimension_semantics=sem, vmem_limit_bytes=VMEM_LIMIT)


def _layer_norm(y, g, b):
    mu = jnp.mean(y, axis=-1, keepdims=True)
    d = y - mu
    var = jnp.mean(d * d, axis=-1, keepdims=True)
    return d * lax.rsqrt(var + LN_EPS) * g + b


def _const_spec(shape):
    nd = len(shape)
    return pl.BlockSpec(shape, lambda *_: (0,) * nd, pipeline_mode=pl.Buffered(1))


def _ffn_body(x_ref, w1_ref, w3_ref, w2_ref, g_ref, b_ref, acc_ref):
    x = x_ref[...]
    xb = x.astype(BF16)
    acc_ref[...] = jnp.zeros_like(acc_ref)
    for c in range(w1_ref.shape[1] // FF_CHUNK):
        cols = slice(c * FF_CHUNK, (c + 1) * FF_CHUNK)
        a = jnp.dot(xb, w1_ref[:, cols].astype(BF16), preferred_element_type=F32)
        gt = jnp.dot(xb, w3_ref[:, cols].astype(BF16), preferred_element_type=F32)
        act = (a * jax.nn.sigmoid(a)) * gt
        acc_ref[...] += jnp.dot(act.astype(BF16), w2_ref[cols, :].astype(BF16),
                                preferred_element_type=F32)
    y = DN_ALPHA * x + 0.5 * acc_ref[...]
    return _layer_norm(y, g_ref[...], b_ref[...])


def _ffn_kernel(x_ref, w1_ref, w3_ref, w2_ref, g_ref, b_ref, o_ref, acc_ref):
    o_ref[...] = _ffn_body(x_ref, w1_ref, w3_ref, w2_ref, g_ref, b_ref, acc_ref)


def _ffn_ple_kernel(x_ref, w1_ref, w3_ref, w2_ref, g_ref, b_ref, p_ref, wp_ref, wg_ref,
                    o_ref, acc_ref):
    h = _ffn_body(x_ref, w1_ref, w3_ref, w2_ref, g_ref, b_ref, acc_ref)
    gate = jax.nn.sigmoid(jnp.dot(h.astype(BF16), wg_ref[...], preferred_element_type=F32))
    pe = jnp.dot(p_ref[...].astype(BF16), wp_ref[...], preferred_element_type=F32)
    o_ref[...] = h + pe * gate


def _ffn_ln(x, w1, w3, w2, layer, half, g, b, ple=None):
    m, d = x.shape
    row = pl.BlockSpec((FFN_ROWS, d), lambda i: (i, 0))

    def whole(w):
        return pl.BlockSpec((None, None) + w.shape[2:], lambda i: (layer, half, 0, 0),
                            pipeline_mode=pl.Buffered(1))

    in_specs = [row, whole(w1), whole(w3), whole(w2), _const_spec((1, d)), _const_spec((1, d))]
    args = [x, w1, w3, w2, g, b]
    body = _ffn_kernel
    name = "ffn_ln"
    if ple is not None:
        p, wp, wg = ple
        in_specs += [pl.BlockSpec((FFN_ROWS, p.shape[1]), lambda i: (i, 0)),
                     _const_spec(wp.shape), _const_spec(wg.shape)]
        args += [p, wp, wg]
        body = _ffn_ple_kernel
        name = "ffn_ln_ple"
    return pl.pallas_call(
        body,
        out_shape=jax.ShapeDtypeStruct((m, d), F32),
        grid=(m // FFN_ROWS,),
        in_specs=in_specs,
        out_specs=row,
        scratch_shapes=[pltpu.VMEM((FFN_ROWS, d), F32)],
        compiler_params=_cparams("parallel"),
        name=name,
    )(*args)


def _ab_proj_kernel(x_ref, w_ref, cg_ref, u_ref):
    y = jnp.dot(x_ref[...].astype(BF16), w_ref[...], preferred_element_type=F32)
    n_cg = cg_ref.shape[1]
    cg_ref[...] = y[:, :n_cg]
    u_ref[...] = y[:, n_cg:]


def _ab_proj(x, w):
    m, d = x.shape
    n = w.shape[1]
    n_cg = 3 * CONV_CH
    return pl.pallas_call(
        _ab_proj_kernel,
        out_shape=(jax.ShapeDtypeStruct((m, n_cg), F32), jax.ShapeDtypeStruct((m, n - n_cg), F32)),
        grid=(m // PROJ_ROWS,),
        in_specs=[pl.BlockSpec((PROJ_ROWS, d), lambda i: (i, 0)), _const_spec((d, n))],
        out_specs=(pl.BlockSpec((PROJ_ROWS, n_cg), lambda i: (i, 0)),
                   pl.BlockSpec((PROJ_ROWS, n - n_cg), lambda i: (i, 0))),
        compiler_params=_cparams("parallel"),
        name="ab_in_proj",
    )(x, w)


def _s5_tables(lam_re, lam_im, log_dt, b_re, b_im, c_re, c_im):
    lr = jnp.minimum(lam_re, -1e-4)
    li = lam_im
    dt = jnp.exp(log_dt)[:, None]
    mag = jnp.exp(lr * dt)
    ab_re = mag * jnp.cos(li * dt)
    ab_im = mag * jnp.sin(li * dt)
    nr, ni = ab_re - 1.0, ab_im
    den = lr * lr + li * li
    f_re = (nr * lr + ni * li) / den
    f_im = (ni * lr - nr * li) / den
    bb_re = f_re[..., None] * b_re - f_im[..., None] * b_im
    bb_im = f_re[..., None] * b_im + f_im[..., None] * b_re
    pw_re = [jnp.ones_like(ab_re)]
    pw_im = [jnp.zeros_like(ab_im)]
    for _ in range(S5_T):
        pr, pi = pw_re[-1], pw_im[-1]
        pw_re.append(pr * ab_re - pi * ab_im)
        pw_im.append(pr * ab_im + pi * ab_re)
    pw_re = jnp.stack(pw_re)
    pw_im = jnp.stack(pw_im)
    cp_re = c_re[None] * pw_re[:, :, None, :] - c_im[None] * pw_im[:, :, None, :]
    cp_im = c_re[None] * pw_im[:, :, None, :] + c_im[None] * pw_re[:, :, None, :]
    taps = (jnp.einsum('ngcp,gpd->ngcd', cp_re[:S5_T], bb_re, precision=HIGHEST)
            - jnp.einsum('ngcp,gpd->ngcd', cp_im[:S5_T], bb_im, precision=HIGHEST))
    s_idx = jnp.arange(S5_T)[:, None]
    t_idx = jnp.arange(S5_T)[None, :]
    lag = t_idx - s_idx
    k_st = taps[jnp.clip(lag, 0, S5_T - 1)]
    k_st = jnp.where((lag >= 0)[:, :, None, None, None], k_st, 0.0)
    toep = jnp.transpose(k_st, (2, 0, 4, 1, 3)).reshape(S5_GROUPS, S5_T * S5_GROUP, S5_T * S5_GROUP)
    rev_re = pw_re[S5_T - 1::-1][:S5_T]
    rev_im = pw_im[S5_T - 1::-1][:S5_T]
    win_re = rev_re[..., None] * bb_re[None] - rev_im[..., None] * bb_im[None]
    win_im = rev_re[..., None] * bb_im[None] + rev_im[..., None] * bb_re[None]
    w_in = jnp.concatenate([win_re, win_im], axis=2)
    w_in = jnp.transpose(w_in, (1, 0, 3, 2)).reshape(S5_GROUPS, S5_T * S5_GROUP, 2 * S5_STATE)
    wo_re = jnp.transpose(cp_re[1:], (1, 3, 0, 2))
    wo_im = jnp.transpose(-cp_im[1:], (1, 3, 0, 2))
    w_out = jnp.concatenate([wo_re, wo_im], axis=1).reshape(S5_GROUPS, 2 * S5_STATE, S5_T * S5_GROUP)
    a1 = jnp.concatenate([pw_re[S5_T], pw_re[S5_T]], axis=-1)
    a2 = jnp.concatenate([-pw_im[S5_T], pw_im[S5_T]], axis=-1)
    return toep, w_in, w_out, a1, a2


def _s5_local_kernel(uc_ref, win_ref, o_ref):
    o_ref[0] = jnp.dot(uc_ref[0], win_ref[0], preferred_element_type=F32, precision=HIGHEST)


def _s5_local(uc, w_in):
    g, rows, width = uc.shape
    ns = w_in.shape[2]
    return pl.pallas_call(
        _s5_local_kernel,
        out_shape=jax.ShapeDtypeStruct((g, rows, ns), F32),
        grid=(g,),
        in_specs=[pl.BlockSpec((1, rows, width), lambda i: (i, 0, 0)),
                  pl.BlockSpec((1, width, ns), lambda i: (i, 0, 0))],
        out_specs=pl.BlockSpec((1, rows, ns), lambda i: (i, 0, 0)),
        compiler_params=_cparams("parallel"),
        name="s5_chunk_state_in",
    )(uc, w_in)


def _s5_scan_kernel(g_ref, a1_ref, a2_ref, o_ref):
    a1 = a1_ref[...]
    a2 = a2_ref[...]
    half = a1.shape[1] // 2

    a2s = pltpu.roll(a2, half, 1)

    def step(j, st):
        x, xs = st
        o_ref[j] = x
        g = g_ref[j]
        return a1 * x + a2 * xs + g, a1 * xs + a2s * x + pltpu.roll(g, half, 1)

    zero = jnp.zeros(a1.shape, F32)
    lax.fori_loop(0, g_ref.shape[0], step, (zero, zero), unroll=8)


def _s5_scan(gl, a1, a2):
    nch, rows, ns = gl.shape
    rb = 32
    return pl.pallas_call(
        _s5_scan_kernel,
        out_shape=jax.ShapeDtypeStruct((nch, rows, ns), F32),
        grid=(rows // rb,),
        in_specs=[pl.BlockSpec((nch, rb, ns), lambda i: (0, i, 0)),
                  pl.BlockSpec((rb, ns), lambda i: (i, 0)),
                  pl.BlockSpec((rb, ns), lambda i: (i, 0))],
        out_specs=pl.BlockSpec((nch, rb, ns), lambda i: (0, i, 0)),
        compiler_params=_cparams("parallel"),
        name="s5_chunk_scan",
    )(gl, a1, a2)


def _s5_out_kernel(uc_ref, xp_ref, toep_ref, wout_ref, o_ref):
    y = jnp.dot(uc_ref[0], toep_ref[0], preferred_element_type=F32, precision=HIGHEST)
    y = y + jnp.dot(xp_ref[0], wout_ref[0], preferred_element_type=F32, precision=HIGHEST)
    o_ref[0] = y


def _s5_out(uc, xprev, toep, w_out):
    g, rows, width = uc.shape
    ns = xprev.shape[2]
    return pl.pallas_call(
        _s5_out_kernel,
        out_shape=jax.ShapeDtypeStruct((g, rows, width), F32),
        grid=(g,),
        in_specs=[pl.BlockSpec((1, rows, width), lambda i: (i, 0, 0)),
                  pl.BlockSpec((1, rows, ns), lambda i: (i, 0, 0)),
                  pl.BlockSpec((1, width, width), lambda i: (i, 0, 0)),
                  pl.BlockSpec((1, ns, width), lambda i: (i, 0, 0))],
        out_specs=pl.BlockSpec((1, rows, width), lambda i: (i, 0, 0)),
        compiler_params=_cparams("parallel"),
        name="s5_chunk_out",
    )(uc, xprev, toep, w_out)


def _s5_linear(u, tables, batch, seq):
    toep, w_in, w_out, a1, a2 = tables
    nch = seq // S5_T
    uc = u.reshape(batch, nch, S5_T, S5_GROUPS, S5_GROUP)
    uc = jnp.transpose(uc, (3, 0, 1, 2, 4)).reshape(S5_GROUPS, batch * nch, S5_T * S5_GROUP)
    gl = _s5_local(uc, w_in)
    ns = gl.shape[2]
    gl = jnp.transpose(gl.reshape(S5_GROUPS, batch, nch, ns), (2, 1, 0, 3))
    gl = gl.reshape(nch, batch * S5_GROUPS, ns)
    xprev = _s5_scan(gl, jnp.tile(a1, (batch, 1)), jnp.tile(a2, (batch, 1)))
    xprev = jnp.transpose(xprev.reshape(nch, batch, S5_GROUPS, ns), (2, 1, 0, 3))
    xprev = xprev.reshape(S5_GROUPS, batch * nch, ns)
    yc = _s5_out(uc, xprev, toep, w_out)
    yc = yc.reshape(S5_GROUPS, batch, nch, S5_T, S5_GROUP)
    return jnp.transpose(yc, (1, 2, 3, 0, 4)).reshape(batch * seq, S5_CH)


def _gelu_tanh(x):
    c = math.sqrt(2.0 / math.pi)
    return 0.5 * x * (1.0 + jnp.tanh(c * (x + 0.044715 * (x * x * x))))


def _ab_mix_kernel(h_ref, cg_ref, u_ref, y_ref, cw_ref, d_ref, wglu_ref, bglu_ref, wout_ref,
                   g_ref, b_ref, o_ref, uext_ref):
    tm = h_ref.shape[1]
    cc = CONV_CH
    halo = SUBLANES

    @pl.when(pl.program_id(1) == 0)
    def _():
        uext_ref[0:halo, :] = jnp.zeros((halo, cc), F32)

    cg = cg_ref[0]
    uc = cg[:, 2 * cc:3 * cc] * cg[:, 0:cc]
    uext_ref[halo:halo + tm, :] = uc
    v = (cw_ref[0:1, :] * uext_ref[halo - 2:halo - 2 + tm, :]
         + cw_ref[1:2, :] * uext_ref[halo - 1:halo - 1 + tm, :]
         + cw_ref[2:3, :] * uc)
    ya = cg[:, cc:2 * cc] * v
    uext_ref[0:halo, :] = uc[tm - halo:tm, :]

    y2 = y_ref[0] + d_ref[...] * u_ref[0]
    z = _gelu_tanh(y2)
    gate = jax.nn.sigmoid(jnp.dot(z.astype(BF16), wglu_ref[...], preferred_element_type=F32)
                          + bglu_ref[...])
    yb = z * gate
    mix = (jnp.dot(ya.astype(BF16), wout_ref[0:cc, :], preferred_element_type=F32)
           + jnp.dot(yb.astype(BF16), wout_ref[cc:, :], preferred_element_type=F32))
    o_ref[0] = _layer_norm(DN_ALPHA * h_ref[0] + mix, g_ref[...], b_ref[...])


def _ab_mix(h, cg, u, y, conv_w, d_skip, w_glu, b_glu, w_out, g, b, batch, seq):
    d = h.shape[-1]
    tm = PROJ_ROWS

    def row(width):
        return pl.BlockSpec((1, tm, width), lambda bi, si: (bi, si, 0))

    r3 = lambda a: a.reshape(batch, seq, a.shape[-1])
    out = pl.pallas_call(
        _ab_mix_kernel,
        out_shape=jax.ShapeDtypeStruct((batch, seq, d), F32),
        grid=(batch, seq // tm),
        in_specs=[row(d), row(3 * CONV_CH), row(S5_CH), row(S5_CH),
                  _const_spec(conv_w.shape), _const_spec(d_skip.shape), _const_spec(w_glu.shape),
                  _const_spec(b_glu.shape), _const_spec(w_out.shape),
                  _const_spec(g.shape), _const_spec(b.shape)],
        out_specs=row(d),
        scratch_shapes=[pltpu.VMEM((tm + SUBLANES, CONV_CH), F32)],
        compiler_params=_cparams("parallel", "arbitrary"),
        name="ab_mix_out_ln",
    )(r3(h), r3(cg), r3(u), r3(y), conv_w, d_skip, w_glu, b_glu, w_out, g, b)
    return out.reshape(batch * seq, d)


def _rope_tables(positions, rot_dim, width, repeat):
    half = rot_dim // 2
    inv = ROPE_THETA ** (-jnp.arange(0, rot_dim, 2, dtype=F32) / rot_dim)
    ang = positions.astype(F32)[..., None] * inv
    cos, sin = jnp.cos(ang), jnp.sin(ang)
    lead = cos.shape[:-1]
    ones = jnp.ones(lead + (width - 2 * half,), F32)
    zeros = jnp.zeros(lead + (width - 2 * half,), F32)
    ct = jnp.concatenate([cos, cos, ones], axis=-1)
    st = jnp.concatenate([-sin, sin, zeros], axis=-1)
    return jnp.tile(ct, (1, 1, repeat)), jnp.tile(st, (1, 1, repeat))


def _rotate(t, cos_t, sin_t, half, period):
    lane = lax.broadcasted_iota(jnp.int32, t.shape, 1)
    first = (lane % period) < half
    partner = jnp.where(first, pltpu.roll(t, LANES - half, 1), pltpu.roll(t, half, 1))
    return t * cos_t + partner * sin_t


def _c_proj_kernel(x_ref, w_ref, ca_ref, sa_ref, ci_ref, si_ref,
                   q_ref, k_ref, vt_ref, qi_ref, ki_ref, wit_ref):
    y = jnp.dot(x_ref[...].astype(BF16), w_ref[...], preferred_element_type=F32)
    ca, sa, ci, si = ca_ref[...], sa_ref[...], ci_ref[...], si_ref[...]
    half_a = HEAD_DIM // ROT_FRAC // 2
    half_i = IDX_DIM // ROT_FRAC // 2
    nq = N_HEADS * HEAD_DIM
    nkv = N_KV_HEADS * HEAD_DIM
    nqi = IDX_HEADS * IDX_DIM
    for hd in range(N_HEADS):
        t = _rotate(y[:, hd * LANES:(hd + 1) * LANES], ca, sa, half_a, HEAD_DIM)
        q_ref[:, hd * LANES:(hd + 1) * LANES] = (t * QK_SCALE_LOG2).astype(BF16)
    for hd in range(N_KV_HEADS):
        t = y[:, nq + hd * LANES:nq + (hd + 1) * LANES]
        k_ref[:, hd * LANES:(hd + 1) * LANES] = _rotate(t, ca, sa, half_a, HEAD_DIM).astype(BF16)
    for hd in range(N_KV_HEADS):
        v = y[:, nq + nkv + hd * HEAD_DIM:nq + nkv + (hd + 1) * HEAD_DIM]
        vt_ref[0, hd * VT_ROWS:hd * VT_ROWS + HEAD_DIM, :] = v.T.astype(BF16)
        vt_ref[0, hd * VT_ROWS + HEAD_DIM:(hd + 1) * VT_ROWS, :] = jnp.ones(
            (BF16_ROWS, v.shape[0]), BF16)
    o = nq + 2 * nkv
    for pr in range(nqi // LANES):
        t = y[:, o + pr * LANES:o + (pr + 1) * LANES]
        qi_ref[:, pr * LANES:(pr + 1) * LANES] = _rotate(t, ci, si, half_i, IDX_DIM).astype(BF16)
    o += nqi
    t = y[:, o:o + LANES]
    lane = lax.broadcasted_iota(jnp.int32, t.shape, 1)
    ki_lo = jnp.where(lane < IDX_DIM, _rotate(t, ci, si, half_i, IDX_DIM), 0.0)
    ki_ref[:, 0:LANES] = ki_lo.astype(BF16)
    ki_ref[:, LANES:2 * LANES] = pltpu.roll(ki_lo, IDX_DIM, 1).astype(BF16)
    wit = t.T[IDX_DIM:IDX_DIM + IDX_HEADS, :]
    wit_ref[0] = wit * (IDX_HEADS ** -0.5 * IDX_DIM ** -0.5)


def _c_proj(x, w, ca, sa, ci, si, batch, seq):
    m, d = x.shape
    n = w.shape[1]
    tm = PROJ_ROWS
    spb = seq // tm
    nq = N_HEADS * HEAD_DIM
    nkv = N_KV_HEADS * HEAD_DIM
    nqi = IDX_HEADS * IDX_DIM
    row = lambda width: pl.BlockSpec((tm, width), lambda i: (i, 0))
    col = lambda height: pl.BlockSpec((1, height, tm), lambda i: (i // spb, 0, i % spb))
    return pl.pallas_call(
        _c_proj_kernel,
        out_shape=(jax.ShapeDtypeStruct((m, nq), BF16), jax.ShapeDtypeStruct((m, nkv), BF16),
                   jax.ShapeDtypeStruct((batch, N_KV_HEADS * VT_ROWS, seq), BF16),
                   jax.ShapeDtypeStruct((m, nqi), BF16),
                   jax.ShapeDtypeStruct((m, 2 * LANES), BF16),
                   jax.ShapeDtypeStruct((batch, IDX_HEADS, seq), F32)),
        grid=(m // tm,),
        in_specs=[row(d), _const_spec((d, n)), row(LANES), row(LANES), row(LANES), row(LANES)],
        out_specs=(row(nq), row(nkv), col(N_KV_HEADS * VT_ROWS), row(nqi), row(2 * LANES),
                   col(IDX_HEADS)),
        compiler_params=_cparams("parallel"),
        name="c_in_proj_rope",
    )(x, w, ca, sa, ci, si)


def _ordered_key(s):
    bits = pltpu.bitcast(s, jnp.int32)
    return jnp.where(bits < 0, -(bits ^ jnp.int32(-0x80000000)), bits)


def _unkey(k):
    return pltpu.bitcast(jnp.where(k < 0, (-k) ^ jnp.int32(-0x80000000), k), F32)


def _dsa_kernel(q_ref, qi_ref, wit_ref, k_ref, vt_ref, ki_ref, o_ref,
                key_ref, acc_ref, jcut_ref, sa_ref, sb_ref, *, topk):
    tq, tk, tqa = DSA_TQ, DSA_TK, DSA_TQA
    pos_bits = (k_ref.shape[1] - 1).bit_length()
    rep = N_HEADS // N_KV_HEADS
    t0 = pl.program_id(1) * tq
    nkb = (t0 + tq + tk - 1) // tk
    qpos = t0 + lax.broadcasted_iota(jnp.int32, (1, tq), 1)
    krow = lax.broadcasted_iota(jnp.int32, (tk, 1), 0)
    nt = (((1,), (1,)), ((), ()))

    qi = qi_ref[0]
    wit = wit_ref[0]

    def score_block(kb, carry):
        smax, smin = carry
        start = pl.multiple_of(kb * tk, tk)
        sc = jnp.zeros((tk, tq), F32)
        for hd in range(IDX_HEADS):
            pair, side = divmod(hd, LANES // IDX_DIM)
            kib = ki_ref[0, pl.ds(start, tk), side * LANES:(side + 1) * LANES]
            lg = lax.dot_general(kib, qi[:, pair * LANES:(pair + 1) * LANES], nt,
                                 preferred_element_type=F32)
            sc = sc + jnp.maximum(lg, 0.0) * wit[hd:hd + 1, :]
        causal = (start + krow) <= qpos
        key_ref[pl.ds(start, tk), :] = jnp.where(causal, _ordered_key(sc), KEY_NINF)
        smax = jnp.maximum(smax, jnp.max(jnp.where(causal, sc, -jnp.inf), axis=0, keepdims=True))
        smin = jnp.minimum(smin, jnp.min(jnp.where(causal, sc, jnp.inf), axis=0, keepdims=True))
        return smax, smin

    smax, smin = lax.fori_loop(0, nkb, score_block,
                               (jnp.full((1, tq), -jnp.inf, F32), jnp.full((1, tq), jnp.inf, F32)))

    def count(pred):
        def body(kb, c):
            start = pl.multiple_of(kb * tk, tk)
            m = jnp.where(pred(key_ref[pl.ds(start, tk), :], start), 1.0, 0.0)
            return c + jnp.sum(m.reshape(tk // COUNT_ROWS, COUNT_ROWS, tq), axis=0)
        c = lax.fori_loop(0, nkb, body, jnp.zeros((COUNT_ROWS, tq), F32))
        return jnp.sum(c, axis=0, keepdims=True).astype(jnp.int32)

    def any_query(mask):
        return jnp.max(jnp.where(mask, 1.0, 0.0)) > 0.5

    few = (qpos + 1) <= topk
    lo0 = jnp.where(few, KEY_NINF, _ordered_key(smin))
    hi0 = _ordered_key(smax) + 1
    done0 = jnp.where(few, 1, 0)

    def bisect(st, by_value):
        lo, hi, done = st
        if by_value:
            mid = _ordered_key(0.5 * _unkey(lo) + 0.5 * _unkey(jnp.minimum(hi, -KEY_NINF)))
            mid = jnp.minimum(jnp.maximum(mid, lo + 1), hi - 1)
        else:
            mid = (lo >> 1) + (hi >> 1) + (lo & hi & 1)
        c = count(lambda blk, start: blk >= mid)
        active = (done == 0) & (hi - lo != 1)
        ge = c >= topk
        lo = jnp.where(active & ge, mid, lo)
        hi = jnp.where(active & jnp.logical_not(ge), mid, hi)
        done = jnp.where(active & (c == topk), 1, done)
        return lo, hi, done

    def bis_cond(st):
        lo, hi, done = st
        return any_query((done == 0) & (hi - lo != 1))

    def bis_body(st):
        return bisect(bisect(bisect(st, True), True), False)

    thr, _, done = lax.while_loop(bis_cond, bis_body, (lo0, hi0, done0))

    tied = (done == 0) & (thr > KEY_NINF)
    jcut0 = jnp.where(thr > KEY_NINF, jnp.int32(0x7FFFFFFF), jnp.int32(-1))

    def resolve_ties():
        need = topk - count(lambda blk, start: blk > thr)

        def step(b, ans):
            cand = ans | (jnp.int32(1) << (pos_bits - 1 - b))
            c = count(lambda blk, start: (blk == thr) & ((start + krow) < cand))
            return jnp.where(c < need, cand, ans)

        ans = lax.fori_loop(0, pos_bits, step, jnp.zeros((1, tq), jnp.int32))
        jcut_ref[...] = jnp.where(tied, ans, jcut0)

    jcut_ref[...] = jcut0
    pl.when(any_query(tied))(resolve_ties)
    jcut = jcut_ref[...]

    for u in range(tq // tqa):
        qs = slice(u * tqa, (u + 1) * tqa)
        thr_u, jcut_u = thr[:, qs], jcut[:, qs]
        nkb_u = (t0 + (u + 1) * tqa + tk - 1) // tk
        qg = [jnp.concatenate([q_ref[0, qs, (g * rep + r) * HEAD_DIM:(g * rep + r + 1) * HEAD_DIM]
                               for r in range(rep)], axis=0) for g in range(N_KV_HEADS)]
        acc_ref[...] = jnp.zeros_like(acc_ref)
        ncol = rep * tqa

        def col_reduce(op, x):
            part = op(x.reshape(tk // COUNT_ROWS, COUNT_ROWS, ncol), axis=0)
            return op(part, axis=0, keepdims=True)

        def score_into(s_ref, kb):
            start = pl.multiple_of(kb * tk, tk)
            key = key_ref[pl.ds(start, tk), qs]
            sel = (((key > thr_u) | ((key == thr_u) & ((start + krow) <= jcut_u)))
                   & (key > KEY_NINF))
            sel = jnp.concatenate([sel] * rep, axis=1)
            cmax = []
            for g in range(N_KV_HEADS):
                kblk = k_ref[0, pl.ds(start, tk), g * HEAD_DIM:(g + 1) * HEAD_DIM]
                s = lax.dot_general(kblk, qg[g], nt, preferred_element_type=F32)
                s = jnp.where(sel, s, -jnp.inf)
                s_ref[g] = s
                cmax.append(col_reduce(jnp.max, s))
            return cmax

        def accumulate(s_ref, kb, g, m_prev, m_cur):
            start = pl.multiple_of(kb * tk, tk)
            vtb = vt_ref[0, g * VT_ROWS:(g + 1) * VT_ROWS, pl.ds(start, tk)]
            m_safe = jnp.where(m_cur == -jnp.inf, 0.0, m_cur)
            p = jnp.exp2(s_ref[g] - m_safe)
            alpha = jnp.exp2(m_prev - m_safe)
            acc_ref[g] = alpha * acc_ref[g] + jnp.dot(vtb, p.astype(BF16),
                                                      preferred_element_type=F32)

        def step(s_cur, s_next, kb, st):
            cmax = score_into(s_next, kb + 1)
            new = []
            for g in range(N_KV_HEADS):
                m_prev, m_cur = st[2 * g:2 * g + 2]
                accumulate(s_cur, kb, g, m_prev, m_cur)
                new += [m_cur, jnp.maximum(m_cur, cmax[g])]
            return tuple(new)

        def finish(s_cur, kb, st):
            for g in range(N_KV_HEADS):
                accumulate(s_cur, kb, g, st[2 * g], st[2 * g + 1])

        cmax0 = score_into(sa_ref, 0)
        st0 = ()
        for g in range(N_KV_HEADS):
            st0 += (jnp.full((1, ncol), -jnp.inf, F32), cmax0[g])

        def pair(j, st):
            return step(sb_ref, sa_ref, 2 * j + 1, step(sa_ref, sb_ref, 2 * j, st))

        st = lax.fori_loop(0, (nkb_u - 1) // 2, pair, st0)
        last = nkb_u - 1

        @pl.when(last % 2 == 0)
        def _():
            finish(sa_ref, last, st)

        @pl.when(last % 2 == 1)
        def _():
            finish(sb_ref, last, step(sa_ref, sb_ref, last - 1, st))

        for g in range(N_KV_HEADS):
            acc = acc_ref[g]
            og = acc[:HEAD_DIM] / acc[HEAD_DIM:HEAD_DIM + 1]
            for r in range(rep):
                hd = g * rep + r
                o_ref[0, qs, hd * HEAD_DIM:(hd + 1) * HEAD_DIM] = (
                    og[:, r * tqa:(r + 1) * tqa].T.astype(BF16))


def _dsa(q, k, vt, qi, ki, wit, batch, seq):
    topk = min(TOPK_MAX, seq // 4)
    tq = DSA_TQ
    rep = N_HEADS // N_KV_HEADS
    r3 = lambda a: a.reshape(batch, seq, a.shape[-1])
    qrow = lambda width: pl.BlockSpec((1, tq, width), lambda b, i: (b, i, 0))
    full = lambda width: pl.BlockSpec((1, seq, width), lambda b, i: (b, 0, 0))
    out = pl.pallas_call(
        functools.partial(_dsa_kernel, topk=topk),
        out_shape=jax.ShapeDtypeStruct((batch, seq, N_HEADS * HEAD_DIM), BF16),
        grid=(batch, seq // tq),
        in_specs=[qrow(q.shape[-1]), qrow(qi.shape[-1]),
                  pl.BlockSpec((1, wit.shape[1], tq), lambda b, i: (b, 0, i)),
                  full(k.shape[-1]),
                  pl.BlockSpec((1, vt.shape[1], seq), lambda b, i: (b, 0, 0)),
                  full(ki.shape[-1])],
        out_specs=qrow(N_HEADS * HEAD_DIM),
        scratch_shapes=[pltpu.VMEM((seq, tq), jnp.int32),
                        pltpu.VMEM((N_KV_HEADS, VT_ROWS, rep * DSA_TQA), F32),
                        pltpu.VMEM((1, tq), jnp.int32),
                        pltpu.VMEM((N_KV_HEADS, DSA_TK, rep * DSA_TQA), F32),
                        pltpu.VMEM((N_KV_HEADS, DSA_TK, rep * DSA_TQA), F32)],
        compiler_params=_cparams("parallel", "arbitrary"),
        name="dsa_attention",
    )(r3(q), r3(qi), wit, r3(k), vt, r3(ki))
    return out.reshape(batch * seq, N_HEADS * HEAD_DIM)


def _out_ln_kernel(h_ref, a_ref, w_ref, g_ref, b_ref, o_ref):
    mix = jnp.dot(a_ref[...], w_ref[...], preferred_element_type=F32)
    o_ref[...] = _layer_norm(DN_ALPHA * h_ref[...] + mix, g_ref[...], b_ref[...])


def _out_ln(h, a, w, g, b):
    m, d = h.shape
    tm = PROJ_ROWS
    return pl.pallas_call(
        _out_ln_kernel,
        out_shape=jax.ShapeDtypeStruct((m, d), F32),
        grid=(m // tm,),
        in_specs=[pl.BlockSpec((tm, d), lambda i: (i, 0)),
                  pl.BlockSpec((tm, a.shape[1]), lambda i: (i, 0)),
                  _const_spec(w.shape), _const_spec(g.shape), _const_spec(b.shape)],
        out_specs=pl.BlockSpec((tm, d), lambda i: (i, 0)),
        compiler_params=_cparams("parallel"),
        name="c_out_proj_ln",
    )(h, a, w, g, b)


def kernel(x, p, positions, ln_g, ln_b, ffn_w1, ffn_w3, ffn_w2, ple_w_proj, ple_w_gate,
           ab_w_in, ab_w_out, conv_w, s5_lam_re, s5_lam_im, s5_log_dt, s5_b_re, s5_b_im,
           s5_c_re, s5_c_im, s5_d, s5_w_glu, s5_b_glu, c_w_in, c_w_out):
    batch, seq, d = x.shape
    m = batch * seq
    depth = ffn_w1.shape[0]
    g3 = ln_g.reshape(depth, 3, 1, d)
    b3 = ln_b.reshape(depth, 3, 1, d)
    wp = ple_w_proj.astype(BF16)
    wg = ple_w_gate.astype(BF16)
    p2 = p.reshape(depth, m, p.shape[-1])

    pos = positions
    ca, sa = _rope_tables(pos, HEAD_DIM // ROT_FRAC, HEAD_DIM, 1)
    ci, si = _rope_tables(pos, IDX_DIM // ROT_FRAC, IDX_DIM, LANES // IDX_DIM)
    ca, sa, ci, si = (t.reshape(m, LANES) for t in (ca, sa, ci, si))
    c_in = c_w_in.shape[-1]
    c_pad = -c_in % LANES
    cw_in = jnp.pad(c_w_in, ((0, 0), (0, 0), (0, c_pad))).astype(BF16)

    h = x.reshape(m, d)
    for i in range(depth):
        h = _ffn_ln(h, ffn_w1, ffn_w3, ffn_w2, i, 0, g3[i, 0], b3[i, 0])
        j = i // 2
        if i % 2 == 0:
            cg, u = _ab_proj(h, ab_w_in[j].astype(BF16))
            tables = _s5_tables(s5_lam_re[j], s5_lam_im[j], s5_log_dt[j], s5_b_re[j], s5_b_im[j],
                                s5_c_re[j], s5_c_im[j])
            y = _s5_linear(u, tables, batch, seq)
            h = _ab_mix(h, cg, u, y, conv_w[j], s5_d[j].reshape(1, -1), s5_w_glu[j].astype(BF16),
                        s5_b_glu[j].reshape(1, -1), ab_w_out[j].astype(BF16), g3[i, 1], b3[i, 1],
                        batch, seq)
        else:
            q, k, vt, qi, ki, wit = _c_proj(h, cw_in[j], ca, sa, ci, si, batch, seq)
            a = _dsa(q, k, vt, qi, ki, wit, batch, seq)
            h = _out_ln(h, a, c_w_out[j].astype(BF16), g3[i, 1], b3[i, 1])
        h = _ffn_ln(h, ffn_w1, ffn_w3, ffn_w2, i, 1, g3[i, 2], b3[i, 2],
                    ple=(p2[i], wp[i], wg[i]))
    return h.reshape(batch, seq, d)
```

```python
import functools
import math

import jax
import jax.numpy as jnp
from jax import lax
from jax.experimental import pallas as pl
from jax.experimental.pallas import tpu as pltpu

F32 = jnp.float32
BF16 = jnp.bfloat16
HIGHEST = lax.Precision.HIGHEST

DEPTH = 4
LN_EPS = 1e-5
DN_ALPHA = (2 * DEPTH) ** 0.25
CONV_WIDTH = 3
CONV_CH = 512
S5_CH = 512
S5_GROUP = 16
S5_GROUPS = S5_CH // S5_GROUP
S5_STATE = 64
N_HEADS = 8
HEAD_DIM = 128
N_KV_HEADS = 2
IDX_HEADS = 8
IDX_DIM = 64
TOPK_MAX = 256
ROPE_THETA = 500000.0
ROT_FRAC = 4

LANES = 128
SUBLANES = 8
VMEM_LIMIT = 60 * 1024 * 1024

FFN_ROWS = 512
FF_CHUNK = 256
PROJ_ROWS = 512
S5_T = 16
DSA_TQ = 256
DSA_TQA = 128
DSA_TK = 512
COUNT_ROWS = 64
SNAP_AFTER = 4
BF16_ROWS = 16
VT_ROWS = HEAD_DIM + BF16_ROWS
QK_SCALE_LOG2 = HEAD_DIM ** -0.5 * math.log2(math.e)
KEY_NINF = -0x7F800000


def _cparams(*sem):
    return pltpu.CompilerParams(dimension_semantics=sem, vmem_limit_bytes=VMEM_LIMIT)


def _layer_norm(y, g, b):
    mu = jnp.mean(y, axis=-1, keepdims=True)
    d = y - mu
    var = jnp.mean(d * d, axis=-1, keepdims=True)
    return d * lax.rsqrt(var + LN_EPS) * g + b


def _const_spec(shape):
    nd = len(shape)
    return pl.BlockSpec(shape, lambda *_: (0,) * nd, pipeline_mode=pl.Buffered(1))


def _ffn_body(x_ref, w1_ref, w3_ref, w2_ref, g_ref, b_ref, acc_ref):
    x = x_ref[...]
    xb = x.astype(BF16)
    acc_ref[...] = jnp.zeros_like(acc_ref)
    for c in range(w1_ref.shape[1] // FF_CHUNK):
        cols = slice(c * FF_CHUNK, (c + 1) * FF_CHUNK)
        a = jnp.dot(xb, w1_ref[:, cols].astype(BF16), preferred_element_type=F32)
        gt = jnp.dot(xb, w3_ref[:, cols].astype(BF16), preferred_element_type=F32)
        act = (a * jax.nn.sigmoid(a)) * gt
        acc_ref[...] += jnp.dot(act.astype(BF16), w2_ref[cols, :].astype(BF16),
                                preferred_element_type=F32)
    y = DN_ALPHA * x + 0.5 * acc_ref[...]
    return _layer_norm(y, g_ref[...], b_ref[...])


def _ffn_kernel(x_ref, w1_ref, w3_ref, w2_ref, g_ref, b_ref, o_ref, acc_ref):
    o_ref[...] = _ffn_body(x_ref, w1_ref, w3_ref, w2_ref, g_ref, b_ref, acc_ref)


def _ffn_ple_kernel(x_ref, w1_ref, w3_ref, w2_ref, g_ref, b_ref, p_ref, wp_ref, wg_ref,
                    o_ref, acc_ref):
    h = _ffn_body(x_ref, w1_ref, w3_ref, w2_ref, g_ref, b_ref, acc_ref)
    gate = jax.nn.sigmoid(jnp.dot(h.astype(BF16), wg_ref[...], preferred_element_type=F32))
    pe = jnp.dot(p_ref[...].astype(BF16), wp_ref[...], preferred_element_type=F32)
    o_ref[...] = h + pe * gate


def _ffn_ln(x, w1, w3, w2, layer, half, g, b, ple=None):
    m, d = x.shape
    row = pl.BlockSpec((FFN_ROWS, d), lambda i: (i, 0))

    def whole(w):
        return pl.BlockSpec((None, None) + w.shape[2:], lambda i: (layer, half, 0, 0),
                            pipeline_mode=pl.Buffered(1))

    in_specs = [row, whole(w1), whole(w3), whole(w2), _const_spec((1, d)), _const_spec((1, d))]
    args = [x, w1, w3, w2, g, b]
    body = _ffn_kernel
    name = "ffn_ln"
    if ple is not None:
        p, wp, wg = ple
        in_specs += [pl.BlockSpec((FFN_ROWS, p.shape[1]), lambda i: (i, 0)),
                     _const_spec(wp.shape), _const_spec(wg.shape)]
        args += [p, wp, wg]
        body = _ffn_ple_kernel
        name = "ffn_ln_ple"
    return pl.pallas_call(
        body,
        out_shape=jax.ShapeDtypeStruct((m, d), F32),
        grid=(m // FFN_ROWS,),
        in_specs=in_specs,
        out_specs=row,
        scratch_shapes=[pltpu.VMEM((FFN_ROWS, d), F32)],
        compiler_params=_cparams("parallel"),
        name=name,
    )(*args)


def _ab_proj_kernel(x_ref, w_ref, cg_ref, u_ref):
    y = jnp.dot(x_ref[...].astype(BF16), w_ref[...], preferred_element_type=F32)
    n_cg = cg_ref.shape[1]
    cg_ref[...] = y[:, :n_cg]
    u_ref[...] = y[:, n_cg:]


def _ab_proj(x, w):
    m, d = x.shape
    n = w.shape[1]
    n_cg = 3 * CONV_CH
    return pl.pallas_call(
        _ab_proj_kernel,
        out_shape=(jax.ShapeDtypeStruct((m, n_cg), F32), jax.ShapeDtypeStruct((m, n - n_cg), F32)),
        grid=(m // PROJ_ROWS,),
        in_specs=[pl.BlockSpec((PROJ_ROWS, d), lambda i: (i, 0)), _const_spec((d, n))],
        out_specs=(pl.BlockSpec((PROJ_ROWS, n_cg), lambda i: (i, 0)),
                   pl.BlockSpec((PROJ_ROWS, n - n_cg), lambda i: (i, 0))),
        compiler_params=_cparams("parallel"),
        name="ab_in_proj",
    )(x, w)


def _s5_tables(lam_re, lam_im, log_dt, b_re, b_im, c_re, c_im):
    lr = jnp.minimum(lam_re, -1e-4)
    li = lam_im
    dt = jnp.exp(log_dt)[:, None]
    mag = jnp.exp(lr * dt)
    ab_re = mag * jnp.cos(li * dt)
    ab_im = mag * jnp.sin(li * dt)
    nr, ni = ab_re - 1.0, ab_im
    den = lr * lr + li * li
    f_re = (nr * lr + ni * li) / den
    f_im = (ni * lr - nr * li) / den
    bb_re = f_re[..., None] * b_re - f_im[..., None] * b_im
    bb_im = f_re[..., None] * b_im + f_im[..., None] * b_re
    pw_re = [jnp.ones_like(ab_re)]
    pw_im = [jnp.zeros_like(ab_im)]
    for _ in range(S5_T):
        pr, pi = pw_re[-1], pw_im[-1]
        pw_re.append(pr * ab_re - pi * ab_im)
        pw_im.append(pr * ab_im + pi * ab_re)
    pw_re = jnp.stack(pw_re)
    pw_im = jnp.stack(pw_im)
    cp_re = c_re[None] * pw_re[:, :, None, :] - c_im[None] * pw_im[:, :, None, :]
    cp_im = c_re[None] * pw_im[:, :, None, :] + c_im[None] * pw_re[:, :, None, :]
    taps = (jnp.einsum('ngcp,gpd->ngcd', cp_re[:S5_T], bb_re, precision=HIGHEST)
            - jnp.einsum('ngcp,gpd->ngcd', cp_im[:S5_T], bb_im, precision=HIGHEST))
    s_idx = jnp.arange(S5_T)[:, None]
    t_idx = jnp.arange(S5_T)[None, :]
    lag = t_idx - s_idx
    k_st = taps[jnp.clip(lag, 0, S5_T - 1)]
    k_st = jnp.where((lag >= 0)[:, :, None, None, None], k_st, 0.0)
    toep = jnp.transpose(k_st, (2, 0, 4, 1, 3)).reshape(S5_GROUPS, S5_T * S5_GROUP, S5_T * S5_GROUP)
    rev_re = pw_re[S5_T - 1::-1][:S5_T]
    rev_im = pw_im[S5_T - 1::-1][:S5_T]
    win_re = rev_re[..., None] * bb_re[None] - rev_im[..., None] * bb_im[None]
    win_im = rev_re[..., None] * bb_im[None] + rev_im[..., None] * bb_re[None]
    w_in = jnp.concatenate([win_re, win_im], axis=2)
    w_in = jnp.transpose(w_in, (1, 0, 3, 2)).reshape(S5_GROUPS, S5_T * S5_GROUP, 2 * S5_STATE)
    wo_re = jnp.transpose(cp_re[1:], (1, 3, 0, 2))
    wo_im = jnp.transpose(-cp_im[1:], (1, 3, 0, 2))
    w_out = jnp.concatenate([wo_re, wo_im], axis=1).reshape(S5_GROUPS, 2 * S5_STATE, S5_T * S5_GROUP)
    a1 = jnp.concatenate([pw_re[S5_T], pw_re[S5_T]], axis=-1)
    a2 = jnp.concatenate([-pw_im[S5_T], pw_im[S5_T]], axis=-1)
    return toep, w_in, w_out, a1, a2


def _s5_local_kernel(uc_ref, win_ref, o_ref):
    o_ref[0] = jnp.dot(uc_ref[0], win_ref[0], preferred_element_type=F32, precision=HIGHEST)


def _s5_local(uc, w_in):
    g, rows, width = uc.shape
    ns = w_in.shape[2]
    return pl.pallas_call(
        _s5_local_kernel,
        out_shape=jax.ShapeDtypeStruct((g, rows, ns), F32),
        grid=(g,),
        in_specs=[pl.BlockSpec((1, rows, width), lambda i: (i, 0, 0)),
                  pl.BlockSpec((1, width, ns), lambda i: (i, 0, 0))],
        out_specs=pl.BlockSpec((1, rows, ns), lambda i: (i, 0, 0)),
        compiler_params=_cparams("parallel"),
        name="s5_chunk_state_in",
    )(uc, w_in)


def _s5_scan_kernel(g_ref, a1_ref, a2_ref, o_ref):
    a1 = a1_ref[...]
    a2 = a2_ref[...]
    half = a1.shape[1] // 2

    a2s = pltpu.roll(a2, half, 1)

    def step(j, st):
        x, xs = st
        o_ref[j] = x
        g = g_ref[j]
        return a1 * x + a2 * xs + g, a1 * xs + a2s * x + pltpu.roll(g, half, 1)

    zero = jnp.zeros(a1.shape, F32)
    lax.fori_loop(0, g_ref.shape[0], step, (zero, zero), unroll=8)


def _s5_scan(gl, a1, a2):
    nch, rows, ns = gl.shape
    rb = 32
    return pl.pallas_call(
        _s5_scan_kernel,
        out_shape=jax.ShapeDtypeStruct((nch, rows, ns), F32),
        grid=(rows // rb,),
        in_specs=[pl.BlockSpec((nch, rb, ns), lambda i: (0, i, 0)),
                  pl.BlockSpec((rb, ns), lambda i: (i, 0)),
                  pl.BlockSpec((rb, ns), lambda i: (i, 0))],
        out_specs=pl.BlockSpec((nch, rb, ns), lambda i: (0, i, 0)),
        compiler_params=_cparams("parallel"),
        name="s5_chunk_scan",
    )(gl, a1, a2)


def _s5_out_kernel(uc_ref, xp_ref, toep_ref, wout_ref, o_ref):
    y = jnp.dot(uc_ref[0], toep_ref[0], preferred_element_type=F32, precision=HIGHEST)
    y = y + jnp.dot(xp_ref[0], wout_ref[0], preferred_element_type=F32, precision=HIGHEST)
    o_ref[0] = y


def _s5_out(uc, xprev, toep, w_out):
    g, rows, width = uc.shape
    ns = xprev.shape[2]
    return pl.pallas_call(
        _s5_out_kernel,
        out_shape=jax.ShapeDtypeStruct((g, rows, width), F32),
        grid=(g,),
        in_specs=[pl.BlockSpec((1, rows, width), lambda i: (i, 0, 0)),
                  pl.BlockSpec((1, rows, ns), lambda i: (i, 0, 0)),
                  pl.BlockSpec((1, width, width), lambda i: (i, 0, 0)),
                  pl.BlockSpec((1, ns, width), lambda i: (i, 0, 0))],
        out_specs=pl.BlockSpec((1, rows, width), lambda i: (i, 0, 0)),
        compiler_params=_cparams("parallel"),
        name="s5_chunk_out",
    )(uc, xprev, toep, w_out)


def _s5_linear(u, tables, batch, seq):
    toep, w_in, w_out, a1, a2 = tables
    nch = seq // S5_T
    uc = u.reshape(batch, nch, S5_T, S5_GROUPS, S5_GROUP)
    uc = jnp.transpose(uc, (3, 0, 1, 2, 4)).reshape(S5_GROUPS, batch * nch, S5_T * S5_GROUP)
    gl = _s5_local(uc, w_in)
    ns = gl.shape[2]
    gl = jnp.transpose(gl.reshape(S5_GROUPS, batch, nch, ns), (2, 1, 0, 3))
    gl = gl.reshape(nch, batch * S5_GROUPS, ns)
    xprev = _s5_scan(gl, jnp.tile(a1, (batch, 1)), jnp.tile(a2, (batch, 1)))
    xprev = jnp.transpose(xprev.reshape(nch, batch, S5_GROUPS, ns), (2, 1, 0, 3))
    xprev = xprev.reshape(S5_GROUPS, batch * nch, ns)
    yc = _s5_out(uc, xprev, toep, w_out)
    yc = yc.reshape(S5_GROUPS, batch, nch, S5_T, S5_GROUP)
    return jnp.transpose(yc, (1, 2, 3, 0, 4)).reshape(batch * seq, S5_CH)


def _gelu_tanh(x):
    c = math.sqrt(2.0 / math.pi)
    return 0.5 * x * (1.0 + jnp.tanh(c * (x + 0.044715 * (x * x * x))))


def _ab_mix_kernel(h_ref, cg_ref, u_ref, y_ref, cw_ref, d_ref, wglu_ref, bglu_ref, wout_ref,
                   g_ref, b_ref, o_ref, uext_ref):
    tm = h_ref.shape[1]
    cc = CONV_CH
    halo = SUBLANES

    @pl.when(pl.program_id(1) == 0)
    def _():
        uext_ref[0:halo, :] = jnp.zeros((halo, cc), F32)

    cg = cg_ref[0]
    uc = cg[:, 2 * cc:3 * cc] * cg[:, 0:cc]
    uext_ref[halo:halo + tm, :] = uc
    v = (cw_ref[0:1, :] * uext_ref[halo - 2:halo - 2 + tm, :]
         + cw_ref[1:2, :] * uext_ref[halo - 1:halo - 1 + tm, :]
         + cw_ref[2:3, :] * uc)
    ya = cg[:, cc:2 * cc] * v
    uext_ref[0:halo, :] = uc[tm - halo:tm, :]

    y2 = y_ref[0] + d_ref[...] * u_ref[0]
    z = _gelu_tanh(y2)
    gate = jax.nn.sigmoid(jnp.dot(z.astype(BF16), wglu_ref[...], preferred_element_type=F32)
                          + bglu_ref[...])
    yb = z * gate
    mix = (jnp.dot(ya.astype(BF16), wout_ref[0:cc, :], preferred_element_type=F32)
           + jnp.dot(yb.astype(BF16), wout_ref[cc:, :], preferred_element_type=F32))
    o_ref[0] = _layer_norm(DN_ALPHA * h_ref[0] + mix, g_ref[...], b_ref[...])


def _ab_mix(h, cg, u, y, conv_w, d_skip, w_glu, b_glu, w_out, g, b, batch, seq):
    d = h.shape[-1]
    tm = PROJ_ROWS

    def row(width):
        return pl.BlockSpec((1, tm, width), lambda bi, si: (bi, si, 0))

    r3 = lambda a: a.reshape(batch, seq, a.shape[-1])
    out = pl.pallas_call(
        _ab_mix_kernel,
        out_shape=jax.ShapeDtypeStruct((batch, seq, d), F32),
        grid=(batch, seq // tm),
        in_specs=[row(d), row(3 * CONV_CH), row(S5_CH), row(S5_CH),
                  _const_spec(conv_w.shape), _const_spec(d_skip.shape), _const_spec(w_glu.shape),
                  _const_spec(b_glu.shape), _const_spec(w_out.shape),
                  _const_spec(g.shape), _const_spec(b.shape)],
        out_specs=row(d),
        scratch_shapes=[pltpu.VMEM((tm + SUBLANES, CONV_CH), F32)],
        compiler_params=_cparams("parallel", "arbitrary"),
        name="ab_mix_out_ln",
    )(r3(h), r3(cg), r3(u), r3(y), conv_w, d_skip, w_glu, b_glu, w_out, g, b)
    return out.reshape(batch * seq, d)


def _rope_tables(positions, rot_dim, width, repeat):
    half = rot_dim // 2
    inv = ROPE_THETA ** (-jnp.arange(0, rot_dim, 2, dtype=F32) / rot_dim)
    ang = positions.astype(F32)[..., None] * inv
    cos, sin = jnp.cos(ang), jnp.sin(ang)
    lead = cos.shape[:-1]
    ones = jnp.ones(lead + (width - 2 * half,), F32)
    zeros = jnp.zeros(lead + (width - 2 * half,), F32)
    ct = jnp.concatenate([cos, cos, ones], axis=-1)
    st = jnp.concatenate([-sin, sin, zeros], axis=-1)
    return jnp.tile(ct, (1, 1, repeat)), jnp.tile(st, (1, 1, repeat))


def _rotate(t, cos_t, sin_t, half, period):
    lane = lax.broadcasted_iota(jnp.int32, t.shape, 1)
    first = (lane % period) < half
    partner = jnp.where(first, pltpu.roll(t, LANES - half, 1), pltpu.roll(t, half, 1))
    return t * cos_t + partner * sin_t


def _c_proj_kernel(x_ref, w_ref, ca_ref, sa_ref, ci_ref, si_ref,
                   q_ref, k_ref, vt_ref, qi_ref, ki_ref, wit_ref):
    y = jnp.dot(x_ref[...].astype(BF16), w_ref[...], preferred_element_type=F32)
    ca, sa, ci, si = ca_ref[...], sa_ref[...], ci_ref[...], si_ref[...]
    half_a = HEAD_DIM // ROT_FRAC // 2
    half_i = IDX_DIM // ROT_FRAC // 2
    nq = N_HEADS * HEAD_DIM
    nkv = N_KV_HEADS * HEAD_DIM
    nqi = IDX_HEADS * IDX_DIM
    for hd in range(N_HEADS):
        t = _rotate(y[:, hd * LANES:(hd + 1) * LANES], ca, sa, half_a, HEAD_DIM)
        q_ref[:, hd * LANES:(hd + 1) * LANES] = (t * QK_SCALE_LOG2).astype(BF16)
    for hd in range(N_KV_HEADS):
        t = y[:, nq + hd * LANES:nq + (hd + 1) * LANES]
        k_ref[:, hd * LANES:(hd + 1) * LANES] = _rotate(t, ca, sa, half_a, HEAD_DIM).astype(BF16)
    for hd in range(N_KV_HEADS):
        v = y[:, nq + nkv + hd * HEAD_DIM:nq + nkv + (hd + 1) * HEAD_DIM]
        vt_ref[0, hd * VT_ROWS:hd * VT_ROWS + HEAD_DIM, :] = v.T.astype(BF16)
        vt_ref[0, hd * VT_ROWS + HEAD_DIM:(hd + 1) * VT_ROWS, :] = jnp.ones(
            (BF16_ROWS, v.shape[0]), BF16)
    o = nq + 2 * nkv
    for pr in range(nqi // LANES):
        t = y[:, o + pr * LANES:o + (pr + 1) * LANES]
        qi_ref[:, pr * LANES:(pr + 1) * LANES] = _rotate(t, ci, si, half_i, IDX_DIM).astype(BF16)
    o += nqi
    t = y[:, o:o + LANES]
    lane = lax.broadcasted_iota(jnp.int32, t.shape, 1)
    ki_lo = jnp.where(lane < IDX_DIM, _rotate(t, ci, si, half_i, IDX_DIM), 0.0)
    ki_ref[:, 0:LANES] = ki_lo.astype(BF16)
    ki_ref[:, LANES:2 * LANES] = pltpu.roll(ki_lo, IDX_DIM, 1).astype(BF16)
    wit = t.T[IDX_DIM:IDX_DIM + IDX_HEADS, :]
    wit_ref[0] = wit * (IDX_HEADS ** -0.5 * IDX_DIM ** -0.5)


def _c_proj(x, w, ca, sa, ci, si, batch, seq):
    m, d = x.shape
    n = w.shape[1]
    tm = PROJ_ROWS
    spb = seq // tm
    nq = N_HEADS * HEAD_DIM
    nkv = N_KV_HEADS * HEAD_DIM
    nqi = IDX_HEADS * IDX_DIM
    row = lambda width: pl.BlockSpec((tm, width), lambda i: (i, 0))
    col = lambda height: pl.BlockSpec((1, height, tm), lambda i: (i // spb, 0, i % spb))
    return pl.pallas_call(
        _c_proj_kernel,
        out_shape=(jax.ShapeDtypeStruct((m, nq), BF16), jax.ShapeDtypeStruct((m, nkv), BF16),
                   jax.ShapeDtypeStruct((batch, N_KV_HEADS * VT_ROWS, seq), BF16),
                   jax.ShapeDtypeStruct((m, nqi), BF16),
                   jax.ShapeDtypeStruct((m, 2 * LANES), BF16),
                   jax.ShapeDtypeStruct((batch, IDX_HEADS, seq), F32)),
        grid=(m // tm,),
        in_specs=[row(d), _const_spec((d, n)), row(LANES), row(LANES), row(LANES), row(LANES)],
        out_specs=(row(nq), row(nkv), col(N_KV_HEADS * VT_ROWS), row(nqi), row(2 * LANES),
                   col(IDX_HEADS)),
        compiler_params=_cparams("parallel"),
        name="c_in_proj_rope",
    )(x, w, ca, sa, ci, si)


def _ordered_key(s):
    bits = pltpu.bitcast(s, jnp.int32)
    return jnp.where(bits < 0, -(bits ^ jnp.int32(-0x80000000)), bits)


def _unkey(k):
    return pltpu.bitcast(jnp.where(k < 0, (-k) ^ jnp.int32(-0x80000000), k), F32)


def _dsa_kernel(q_ref, qi_ref, wit_ref, k_ref, vt_ref, ki_ref, o_ref,
                key_ref, acc_ref, jcut_ref, sa_ref, sb_ref, *, topk):
    tq, tk, tqa = DSA_TQ, DSA_TK, DSA_TQA
    pos_bits = (k_ref.shape[1] - 1).bit_length()
    rep = N_HEADS // N_KV_HEADS
    t0 = pl.program_id(1) * tq
    nkb = (t0 + tq + tk - 1) // tk
    qpos = t0 + lax.broadcasted_iota(jnp.int32, (1, tq), 1)
    krow = lax.broadcasted_iota(jnp.int32, (tk, 1), 0)
    nt = (((1,), (1,)), ((), ()))

    qi = qi_ref[0]
    wit = wit_ref[0]

    def score_block(kb, carry):
        smax, smin = carry
        start = pl.multiple_of(kb * tk, tk)
        sc = jnp.zeros((tk, tq), F32)
        for hd in range(IDX_HEADS):
            pair, side = divmod(hd, LANES // IDX_DIM)
            kib = ki_ref[0, pl.ds(start, tk), side * LANES:(side + 1) * LANES]
            lg = lax.dot_general(kib, qi[:, pair * LANES:(pair + 1) * LANES], nt,
                                 preferred_element_type=F32)
            sc = sc + jnp.maximum(lg, 0.0) * wit[hd:hd + 1, :]
        causal = (start + krow) <= qpos
        key_ref[pl.ds(start, tk), :] = jnp.where(causal, _ordered_key(sc), KEY_NINF)
        smax = jnp.maximum(smax, jnp.max(jnp.where(causal, sc, -jnp.inf), axis=0, keepdims=True))
        smin = jnp.minimum(smin, jnp.min(jnp.where(causal, sc, jnp.inf), axis=0, keepdims=True))
        return smax, smin

    smax, smin = lax.fori_loop(0, nkb, score_block,
                               (jnp.full((1, tq), -jnp.inf, F32), jnp.full((1, tq), jnp.inf, F32)))

    def count(pred):
        def body(kb, c):
            start = pl.multiple_of(kb * tk, tk)
            m = jnp.where(pred(key_ref[pl.ds(start, tk), :], start), 1.0, 0.0)
            return c + jnp.sum(m.reshape(tk // COUNT_ROWS, COUNT_ROWS, tq), axis=0)
        c = lax.fori_loop(0, nkb, body, jnp.zeros((COUNT_ROWS, tq), F32))
        return jnp.sum(c, axis=0, keepdims=True).astype(jnp.int32)

    def any_query(mask):
        return jnp.max(jnp.where(mask, 1.0, 0.0)) > 0.5

    few = (qpos + 1) <= topk
    lo0 = jnp.where(few, KEY_NINF, _ordered_key(smin))
    hi0 = _ordered_key(smax) + 1
    done0 = jnp.where(few, 1, 0)

    def bisect(st, by_value):
        lo, hi, done = st
        if by_value:
            mid = _ordered_key(0.5 * _unkey(lo) + 0.5 * _unkey(jnp.minimum(hi, -KEY_NINF)))
            mid = jnp.minimum(jnp.maximum(mid, lo + 1), hi - 1)
        else:
            mid = (lo >> 1) + (hi >> 1) + (lo & hi & 1)
        c = count(lambda blk, start: blk >= mid)
        active = (done == 0) & (hi - lo != 1)
        ge = c >= topk
        lo = jnp.where(active & ge, mid, lo)
        hi = jnp.where(active & jnp.logical_not(ge), mid, hi)
        done = jnp.where(active & (c == topk), 1, done)
        return lo, hi, done

    def bis_cond(st):
        lo, hi, done = st
        return any_query((done == 0) & (hi - lo != 1))

    def bis_body(st):
        return bisect(bisect(bisect(st, True), True), False)

    def fold_rows(op, x):
        while x.shape[0] > SUBLANES:
            half = x.shape[0] // 2
            x = op(x[:half], x[half:])
        for shift in (4, 2, 1):
            x = op(x, pltpu.roll(x, shift, 0))
        return x[0:1]

    def snap(st):
        lo, hi, done = st
        big = jnp.int32(0x7FFFFFFF)

        def body(kb, carry):
            mlo, mhi = carry
            blk = key_ref[pl.ds(pl.multiple_of(kb * tk, tk), tk), :]
            above = jnp.where(blk >= lo, blk, big).reshape(tk // COUNT_ROWS, COUNT_ROWS, tq)
            below = jnp.where(blk < hi, blk, -big).reshape(tk // COUNT_ROWS, COUNT_ROWS, tq)
            for j in range(tk // COUNT_ROWS):
                mlo = jnp.minimum(mlo, above[j])
                mhi = jnp.maximum(mhi, below[j])
            return mlo, mhi

        mlo, mhi = lax.fori_loop(0, nkb, body, (jnp.full((COUNT_ROWS, tq), big, jnp.int32),
                                                jnp.full((COUNT_ROWS, tq), -big, jnp.int32)))
        active = (done == 0) & (hi - lo != 1)
        lo = jnp.where(active, fold_rows(jnp.minimum, mlo), lo)
        hi = jnp.where(active, fold_rows(jnp.maximum, mhi) + 1, hi)
        return lo, hi, done

    st = lax.fori_loop(0, SNAP_AFTER, lambda _, s: bis_body(s), (lo0, hi0, done0))
    thr, _, done = lax.while_loop(bis_cond, bis_body, snap(st))

    tied = (done == 0) & (thr > KEY_NINF)
    jcut0 = jnp.where(thr > KEY_NINF, jnp.int32(0x7FFFFFFF), jnp.int32(-1))

    def resolve_ties():
        need = topk - count(lambda blk, start: blk > thr)

        def step(b, ans):
            cand = ans | (jnp.int32(1) << (pos_bits - 1 - b))
            c = count(lambda blk, start: (blk == thr) & ((start + krow) < cand))
            return jnp.where(c < need, cand, ans)

        ans = lax.fori_loop(0, pos_bits, step, jnp.zeros((1, tq), jnp.int32))
        jcut_ref[...] = jnp.where(tied, ans, jcut0)

    jcut_ref[...] = jcut0
    pl.when(any_query(tied))(resolve_ties)
    jcut = jcut_ref[...]

    for u in range(tq // tqa):
        qs = slice(u * tqa, (u + 1) * tqa)
        thr_u, jcut_u = thr[:, qs], jcut[:, qs]
        nkb_u = (t0 + (u + 1) * tqa + tk - 1) // tk
        qg = [jnp.concatenate([q_ref[0, qs, (g * rep + r) * HEAD_DIM:(g * rep + r + 1) * HEAD_DIM]
                               for r in range(rep)], axis=0) for g in range(N_KV_HEADS)]
        acc_ref[...] = jnp.zeros_like(acc_ref)
        ncol = rep * tqa

        def col_reduce(op, x):
            part = op(x.reshape(tk // COUNT_ROWS, COUNT_ROWS, ncol), axis=0)
            return op(part, axis=0, keepdims=True)

        def score_into(s_ref, kb):
            start = pl.multiple_of(kb * tk, tk)
            key = key_ref[pl.ds(start, tk), qs]
            sel = (((key > thr_u) | ((key == thr_u) & ((start + krow) <= jcut_u)))
                   & (key > KEY_NINF))
            sel = jnp.concatenate([sel] * rep, axis=1)
            cmax = []
            for g in range(N_KV_HEADS):
                kblk = k_ref[0, pl.ds(start, tk), g * HEAD_DIM:(g + 1) * HEAD_DIM]
                s = lax.dot_general(kblk, qg[g], nt, preferred_element_type=F32)
                s = jnp.where(sel, s, -jnp.inf)
                s_ref[g] = s
                cmax.append(col_reduce(jnp.max, s))
            return cmax

        def accumulate(s_ref, kb, g, m_prev, m_cur):
            start = pl.multiple_of(kb * tk, tk)
            vtb = vt_ref[0, g * VT_ROWS:(g + 1) * VT_ROWS, pl.ds(start, tk)]
            m_safe = jnp.where(m_cur == -jnp.inf, 0.0, m_cur)
            p = jnp.exp2(s_ref[g] - m_safe)
            alpha = jnp.exp2(m_prev - m_safe)
            acc_ref[g] = alpha * acc_ref[g] + jnp.dot(vtb, p.astype(BF16),
                                                      preferred_element_type=F32)

        def step(s_cur, s_next, kb, st):
            cmax = score_into(s_next, kb + 1)
            new = []
            for g in range(N_KV_HEADS):
                m_prev, m_cur = st[2 * g:2 * g + 2]
                accumulate(s_cur, kb, g, m_prev, m_cur)
                new += [m_cur, jnp.maximum(m_cur, cmax[g])]
            return tuple(new)

        def finish(s_cur, kb, st):
            for g in range(N_KV_HEADS):
                accumulate(s_cur, kb, g, st[2 * g], st[2 * g + 1])

        cmax0 = score_into(sa_ref, 0)
        st0 = ()
        for g in range(N_KV_HEADS):
            st0 += (jnp.full((1, ncol), -jnp.inf, F32), cmax0[g])

        def pair(j, st):
            return step(sb_ref, sa_ref, 2 * j + 1, step(sa_ref, sb_ref, 2 * j, st))

        st = lax.fori_loop(0, (nkb_u - 1) // 2, pair, st0)
        last = nkb_u - 1

        @pl.when(last % 2 == 0)
        def _():
            finish(sa_ref, last, st)

        @pl.when(last % 2 == 1)
        def _():
            finish(sb_ref, last, step(sa_ref, sb_ref, last - 1, st))

        for g in range(N_KV_HEADS):
            acc = acc_ref[g]
            og = acc[:HEAD_DIM] / acc[HEAD_DIM:HEAD_DIM + 1]
            for r in range(rep):
                hd = g * rep + r
                o_ref[0, qs, hd * HEAD_DIM:(hd + 1) * HEAD_DIM] = (
                    og[:, r * tqa:(r + 1) * tqa].T.astype(BF16))


def _dsa(q, k, vt, qi, ki, wit, batch, seq):
    topk = min(TOPK_MAX, seq // 4)
    tq = DSA_TQ
    rep = N_HEADS // N_KV_HEADS
    r3 = lambda a: a.reshape(batch, seq, a.shape[-1])
    qrow = lambda width: pl.BlockSpec((1, tq, width), lambda b, i: (b, i, 0))
    full = lambda width: pl.BlockSpec((1, seq, width), lambda b, i: (b, 0, 0))
    out = pl.pallas_call(
        functools.partial(_dsa_kernel, topk=topk),
        out_shape=jax.ShapeDtypeStruct((batch, seq, N_HEADS * HEAD_DIM), BF16),
        grid=(batch, seq // tq),
        in_specs=[qrow(q.shape[-1]), qrow(qi.shape[-1]),
                  pl.BlockSpec((1, wit.shape[1], tq), lambda b, i: (b, 0, i)),
                  full(k.shape[-1]),
                  pl.BlockSpec((1, vt.shape[1], seq), lambda b, i: (b, 0, 0)),
                  full(ki.shape[-1])],
        out_specs=qrow(N_HEADS * HEAD_DIM),
        scratch_shapes=[pltpu.VMEM((seq, tq), jnp.int32),
                        pltpu.VMEM((N_KV_HEADS, VT_ROWS, rep * DSA_TQA), F32),
                        pltpu.VMEM((1, tq), jnp.int32),
                        pltpu.VMEM((N_KV_HEADS, DSA_TK, rep * DSA_TQA), F32),
                        pltpu.VMEM((N_KV_HEADS, DSA_TK, rep * DSA_TQA), F32)],
        compiler_params=_cparams("parallel", "arbitrary"),
        name="dsa_attention",
    )(r3(q), r3(qi), wit, r3(k), vt, r3(ki))
    return out.reshape(batch * seq, N_HEADS * HEAD_DIM)


def _out_ln_kernel(h_ref, a_ref, w_ref, g_ref, b_ref, o_ref):
    mix = jnp.dot(a_ref[...], w_ref[...], preferred_element_type=F32)
    o_ref[...] = _layer_norm(DN_ALPHA * h_ref[...] + mix, g_ref[...], b_ref[...])


def _out_ln(h, a, w, g, b):
    m, d = h.shape
    tm = PROJ_ROWS
    return pl.pallas_call(
        _out_ln_kernel,
        out_shape=jax.ShapeDtypeStruct((m, d), F32),
        grid=(m // tm,),
        in_specs=[pl.BlockSpec((tm, d), lambda i: (i, 0)),
                  pl.BlockSpec((tm, a.shape[1]), lambda i: (i, 0)),
                  _const_spec(w.shape), _const_spec(g.shape), _const_spec(b.shape)],
        out_specs=pl.BlockSpec((tm, d), lambda i: (i, 0)),
        compiler_params=_cparams("parallel"),
        name="c_out_proj_ln",
    )(h, a, w, g, b)


def kernel(x, p, positions, ln_g, ln_b, ffn_w1, ffn_w3, ffn_w2, ple_w_proj, ple_w_gate,
           ab_w_in, ab_w_out, conv_w, s5_lam_re, s5_lam_im, s5_log_dt, s5_b_re, s5_b_im,
           s5_c_re, s5_c_im, s5_d, s5_w_glu, s5_b_glu, c_w_in, c_w_out):
    batch, seq, d = x.shape
    m = batch * seq
    depth = ffn_w1.shape[0]
    g3 = ln_g.reshape(depth, 3, 1, d)
    b3 = ln_b.reshape(depth, 3, 1, d)
    wp = ple_w_proj.astype(BF16)
    wg = ple_w_gate.astype(BF16)
    p2 = p.reshape(depth, m, p.shape[-1])

    pos = positions
    ca, sa = _rope_tables(pos, HEAD_DIM // ROT_FRAC, HEAD_DIM, 1)
    ci, si = _rope_tables(pos, IDX_DIM // ROT_FRAC, IDX_DIM, LANES // IDX_DIM)
    ca, sa, ci, si = (t.reshape(m, LANES) for t in (ca, sa, ci, si))
    c_in = c_w_in.shape[-1]
    c_pad = -c_in % LANES
    cw_in = jnp.pad(c_w_in, ((0, 0), (0, 0), (0, c_pad))).astype(BF16)

    h = x.reshape(m, d)
    for i in range(depth):
        h = _ffn_ln(h, ffn_w1, ffn_w3, ffn_w2, i, 0, g3[i, 0], b3[i, 0])
        j = i // 2
        if i % 2 == 0:
            cg, u = _ab_proj(h, ab_w_in[j].astype(BF16))
            tables = _s5_tables(s5_lam_re[j], s5_lam_im[j], s5_log_dt[j], s5_b_re[j], s5_b_im[j],
                                s5_c_re[j], s5_c_im[j])
            y = _s5_linear(u, tables, batch, seq)
            h = _ab_mix(h, cg, u, y, conv_w[j], s5_d[j].reshape(1, -1), s5_w_glu[j].astype(BF16),
                        s5_b_glu[j].reshape(1, -1), ab_w_out[j].astype(BF16), g3[i, 1], b3[i, 1],
                        batch, seq)
        else:
            q, k, vt, qi, ki, wit = _c_proj(h, cw_in[j], ca, sa, ci, si, batch, seq)
            a = _dsa(q, k, vt, qi, ki, wit, batch, seq)
            h = _out_ln(h, a, c_w_out[j].astype(BF16), g3[i, 1], b3[i, 1])
        h = _ffn_ln(h, ffn_w1, ffn_w3, ffn_w2, i, 1, g3[i, 2], b3[i, 2],
                    ple=(p2[i], wp[i], wg[i]))
    return h.reshape(batch, seq, d)
```

```python
import functools
import math

import jax
import jax.numpy as jnp
from jax import lax
from jax.experimental import pallas as pl
from jax.experimental.pallas import tpu as pltpu

F32 = jnp.float32
BF16 = jnp.bfloat16
HIGHEST = lax.Precision.HIGHEST

DEPTH = 4
LN_EPS = 1e-5
DN_ALPHA = (2 * DEPTH) ** 0.25
CONV_WIDTH = 3
CONV_CH = 512
S5_CH = 512
S5_GROUP = 16
S5_GROUPS = S5_CH // S5_GROUP
S5_STATE = 64
N_HEADS = 8
HEAD_DIM = 128
N_KV_HEADS = 2
IDX_HEADS = 8
IDX_DIM = 64
TOPK_MAX = 256
ROPE_THETA = 500000.0
ROT_FRAC = 4

LANES = 128
SUBLANES = 8
VMEM_LIMIT = 60 * 1024 * 1024

FFN_ROWS = 512
FF_CHUNK = 256
PROJ_ROWS = 512
S5_T = 16
DSA_TQ = 256
DSA_TQA = 128
DSA_TK = 512
COUNT_ROWS = 64
BF16_ROWS = 16
VT_ROWS = HEAD_DIM + BF16_ROWS
QK_SCALE_LOG2 = HEAD_DIM ** -0.5 * math.log2(math.e)
KEY_NINF = -0x7F800000


def _cparams(*sem):
    return pltpu.CompilerParams(dimension_semantics=sem, vmem_limit_bytes=VMEM_LIMIT)


def _layer_norm(y, g, b):
    mu = jnp.mean(y, axis=-1, keepdims=True)
    d = y - mu
    var = jnp.mean(d * d, axis=-1, keepdims=True)
    return d * lax.rsqrt(var + LN_EPS) * g + b


def _const_spec(shape):
    nd = len(shape)
    return pl.BlockSpec(shape, lambda *_: (0,) * nd, pipeline_mode=pl.Buffered(1))


def _ffn_body(x_ref, w1_ref, w3_ref, w2_ref, g_ref, b_ref, acc_ref):
    x = x_ref[...]
    xb = x.astype(BF16)
    acc_ref[...] = jnp.zeros_like(acc_ref)
    for c in range(w1_ref.shape[1] // FF_CHUNK):
        cols = slice(c * FF_CHUNK, (c + 1) * FF_CHUNK)
        a = jnp.dot(xb, w1_ref[:, cols].astype(BF16), preferred_element_type=F32)
        gt = jnp.dot(xb, w3_ref[:, cols].astype(BF16), preferred_element_type=F32)
        act = (a * jax.nn.sigmoid(a)) * gt
        acc_ref[...] += jnp.dot(act.astype(BF16), w2_ref[cols, :].astype(BF16),
                                preferred_element_type=F32)
    y = DN_ALPHA * x + 0.5 * acc_ref[...]
    return _layer_norm(y, g_ref[...], b_ref[...])


def _ffn_kernel(x_ref, w1_ref, w3_ref, w2_ref, g_ref, b_ref, o_ref, acc_ref):
    o_ref[...] = _ffn_body(x_ref, w1_ref, w3_ref, w2_ref, g_ref, b_ref, acc_ref)


def _ffn_ple_kernel(x_ref, w1_ref, w3_ref, w2_ref, g_ref, b_ref, p_ref, wp_ref, wg_ref,
                    o_ref, acc_ref):
    h = _ffn_body(x_ref, w1_ref, w3_ref, w2_ref, g_ref, b_ref, acc_ref)
    gate = jax.nn.sigmoid(jnp.dot(h.astype(BF16), wg_ref[...], preferred_element_type=F32))
    pe = jnp.dot(p_ref[...].astype(BF16), wp_ref[...], preferred_element_type=F32)
    o_ref[...] = h + pe * gate


def _ffn_ln(x, w1, w3, w2, layer, half, g, b, ple=None):
    m, d = x.shape
    row = pl.BlockSpec((FFN_ROWS, d), lambda i: (i, 0))

    def whole(w):
        return pl.BlockSpec((None, None) + w.shape[2:], lambda i: (layer, half, 0, 0),
                            pipeline_mode=pl.Buffered(1))

    in_specs = [row, whole(w1), whole(w3), whole(w2), _const_spec((1, d)), _const_spec((1, d))]
    args = [x, w1, w3, w2, g, b]
    body = _ffn_kernel
    name = "ffn_ln"
    if ple is not None:
        p, wp, wg = ple
        in_specs += [pl.BlockSpec((FFN_ROWS, p.shape[1]), lambda i: (i, 0)),
                     _const_spec(wp.shape), _const_spec(wg.shape)]
        args += [p, wp, wg]
        body = _ffn_ple_kernel
        name = "ffn_ln_ple"
    return pl.pallas_call(
        body,
        out_shape=jax.ShapeDtypeStruct((m, d), F32),
        grid=(m // FFN_ROWS,),
        in_specs=in_specs,
        out_specs=row,
        scratch_shapes=[pltpu.VMEM((FFN_ROWS, d), F32)],
        compiler_params=_cparams("parallel"),
        name=name,
    )(*args)


def _ab_proj_kernel(x_ref, w_ref, cg_ref, u_ref):
    y = jnp.dot(x_ref[...].astype(BF16), w_ref[...], preferred_element_type=F32)
    n_cg = cg_ref.shape[1]
    cg_ref[...] = y[:, :n_cg]
    u_ref[...] = y[:, n_cg:]


def _ab_proj(x, w):
    m, d = x.shape
    n = w.shape[1]
    n_cg = 3 * CONV_CH
    return pl.pallas_call(
        _ab_proj_kernel,
        out_shape=(jax.ShapeDtypeStruct((m, n_cg), F32), jax.ShapeDtypeStruct((m, n - n_cg), F32)),
        grid=(m // PROJ_ROWS,),
        in_specs=[pl.BlockSpec((PROJ_ROWS, d), lambda i: (i, 0)), _const_spec((d, n))],
        out_specs=(pl.BlockSpec((PROJ_ROWS, n_cg), lambda i: (i, 0)),
                   pl.BlockSpec((PROJ_ROWS, n - n_cg), lambda i: (i, 0))),
        compiler_params=_cparams("parallel"),
        name="ab_in_proj",
    )(x, w)


def _s5_tables(lam_re, lam_im, log_dt, b_re, b_im, c_re, c_im):
    lr = jnp.minimum(lam_re, -1e-4)
    li = lam_im
    dt = jnp.exp(log_dt)[:, None]
    mag = jnp.exp(lr * dt)
    ab_re = mag * jnp.cos(li * dt)
    ab_im = mag * jnp.sin(li * dt)
    nr, ni = ab_re - 1.0, ab_im
    den = lr * lr + li * li
    f_re = (nr * lr + ni * li) / den
    f_im = (ni * lr - nr * li) / den
    bb_re = f_re[..., None] * b_re - f_im[..., None] * b_im
    bb_im = f_re[..., None] * b_im + f_im[..., None] * b_re
    pw_re = [jnp.ones_like(ab_re)]
    pw_im = [jnp.zeros_like(ab_im)]
    for _ in range(S5_T):
        pr, pi = pw_re[-1], pw_im[-1]
        pw_re.append(pr * ab_re - pi * ab_im)
        pw_im.append(pr * ab_im + pi * ab_re)
    pw_re = jnp.stack(pw_re)
    pw_im = jnp.stack(pw_im)
    cp_re = c_re[None] * pw_re[:, :, None, :] - c_im[None] * pw_im[:, :, None, :]
    cp_im = c_re[None] * pw_im[:, :, None, :] + c_im[None] * pw_re[:, :, None, :]
    taps = (jnp.einsum('ngcp,gpd->ngcd', cp_re[:S5_T], bb_re, precision=HIGHEST)
            - jnp.einsum('ngcp,gpd->ngcd', cp_im[:S5_T], bb_im, precision=HIGHEST))
    s_idx = jnp.arange(S5_T)[:, None]
    t_idx = jnp.arange(S5_T)[None, :]
    lag = t_idx - s_idx
    k_st = taps[jnp.clip(lag, 0, S5_T - 1)]
    k_st = jnp.where((lag >= 0)[:, :, None, None, None], k_st, 0.0)
    toep = jnp.transpose(k_st, (2, 0, 4, 1, 3)).reshape(S5_GROUPS, S5_T * S5_GROUP, S5_T * S5_GROUP)
    rev_re = pw_re[S5_T - 1::-1][:S5_T]
    rev_im = pw_im[S5_T - 1::-1][:S5_T]
    win_re = rev_re[..., None] * bb_re[None] - rev_im[..., None] * bb_im[None]
    win_im = rev_re[..., None] * bb_im[None] + rev_im[..., None] * bb_re[None]
    w_in = jnp.concatenate([win_re, win_im], axis=2)
    w_in = jnp.transpose(w_in, (1, 0, 3, 2)).reshape(S5_GROUPS, S5_T * S5_GROUP, 2 * S5_STATE)
    wo_re = jnp.transpose(cp_re[1:], (1, 3, 0, 2))
    wo_im = jnp.transpose(-cp_im[1:], (1, 3, 0, 2))
    w_out = jnp.concatenate([wo_re, wo_im], axis=1).reshape(S5_GROUPS, 2 * S5_STATE, S5_T * S5_GROUP)
    a1 = jnp.concatenate([pw_re[S5_T], pw_re[S5_T]], axis=-1)
    a2 = jnp.concatenate([-pw_im[S5_T], pw_im[S5_T]], axis=-1)
    return toep, w_in, w_out, a1, a2


def _s5_local_kernel(uc_ref, win_ref, o_ref):
    o_ref[...] = jnp.dot(uc_ref[...], win_ref[0], preferred_element_type=F32, precision=HIGHEST)


def _s5_local(uc, w_in):
    rows = uc.shape[0]
    g, width, ns = w_in.shape
    return pl.pallas_call(
        _s5_local_kernel,
        out_shape=jax.ShapeDtypeStruct((rows, g * ns), F32),
        grid=(g,),
        in_specs=[pl.BlockSpec((rows, width), lambda i: (0, i)),
                  pl.BlockSpec((1, width, ns), lambda i: (i, 0, 0))],
        out_specs=pl.BlockSpec((rows, ns), lambda i: (0, i)),
        compiler_params=_cparams("parallel"),
        name="s5_chunk_state_in",
    )(uc, w_in)


def _s5_scan_kernel(g_ref, a1_ref, a2_ref, o_ref):
    a1 = a1_ref[...]
    a2 = a2_ref[...]
    half = a1.shape[1] // 2

    a2s = pltpu.roll(a2, half, 1)

    def step(j, st):
        x, xs = st
        o_ref[j] = x
        g = g_ref[j]
        return a1 * x + a2 * xs + g, a1 * xs + a2s * x + pltpu.roll(g, half, 1)

    zero = jnp.zeros(a1.shape, F32)
    lax.fori_loop(0, g_ref.shape[0], step, (zero, zero), unroll=8)


def _s5_scan(gl, a1, a2):
    nch, rows, ns = gl.shape
    rb = 32
    return pl.pallas_call(
        _s5_scan_kernel,
        out_shape=jax.ShapeDtypeStruct((nch, rows, ns), F32),
        grid=(rows // rb,),
        in_specs=[pl.BlockSpec((nch, rb, ns), lambda i: (0, i, 0)),
                  pl.BlockSpec((rb, ns), lambda i: (i, 0)),
                  pl.BlockSpec((rb, ns), lambda i: (i, 0))],
        out_specs=pl.BlockSpec((nch, rb, ns), lambda i: (0, i, 0)),
        compiler_params=_cparams("parallel"),
        name="s5_chunk_scan",
    )(gl, a1, a2)


def _s5_out_kernel(uc_ref, xp_ref, toep_ref, wout_ref, o_ref):
    y = jnp.dot(uc_ref[...], toep_ref[0], preferred_element_type=F32, precision=HIGHEST)
    y = y + jnp.dot(xp_ref[...], wout_ref[0], preferred_element_type=F32, precision=HIGHEST)
    o_ref[...] = y


def _s5_out(uc, xprev, toep, w_out):
    rows = uc.shape[0]
    g, ns, width = w_out.shape
    return pl.pallas_call(
        _s5_out_kernel,
        out_shape=jax.ShapeDtypeStruct((rows, g * width), F32),
        grid=(g,),
        in_specs=[pl.BlockSpec((rows, width), lambda i: (0, i)),
                  pl.BlockSpec((rows, ns), lambda i: (0, i)),
                  pl.BlockSpec((1, width, width), lambda i: (i, 0, 0)),
                  pl.BlockSpec((1, ns, width), lambda i: (i, 0, 0))],
        out_specs=pl.BlockSpec((rows, width), lambda i: (0, i)),
        compiler_params=_cparams("parallel"),
        name="s5_chunk_out",
    )(uc, xprev, toep, w_out)


def _s5_linear(u, tables, batch, seq):
    toep, w_in, w_out, a1, a2 = tables
    nch = seq // S5_T
    ns = w_in.shape[2]
    uc = u.reshape(batch * nch, S5_T, S5_GROUPS, S5_GROUP)
    uc = jnp.transpose(uc, (0, 2, 1, 3)).reshape(batch * nch, S5_GROUPS * S5_T * S5_GROUP)
    gl = _s5_local(uc, w_in)
    gl = jnp.transpose(gl.reshape(batch, nch, S5_GROUPS * ns), (1, 0, 2))
    gl = gl.reshape(nch, batch * S5_GROUPS, ns)
    xprev = _s5_scan(gl, jnp.tile(a1, (batch, 1)), jnp.tile(a2, (batch, 1)))
    xprev = jnp.transpose(xprev.reshape(nch, batch, S5_GROUPS * ns), (1, 0, 2))
    xprev = xprev.reshape(batch * nch, S5_GROUPS * ns)
    yc = _s5_out(uc, xprev, toep, w_out)
    yc = yc.reshape(batch * nch, S5_GROUPS, S5_T, S5_GROUP)
    return jnp.transpose(yc, (0, 2, 1, 3)).reshape(batch * seq, S5_CH)


def _gelu_tanh(x):
    c = math.sqrt(2.0 / math.pi)
    return 0.5 * x * (1.0 + jnp.tanh(c * (x + 0.044715 * (x * x * x))))


def _ab_mix_kernel(h_ref, cg_ref, u_ref, y_ref, cw_ref, d_ref, wglu_ref, bglu_ref, wout_ref,
                   g_ref, b_ref, o_ref, uext_ref):
    tm = h_ref.shape[1]
    cc = CONV_CH
    halo = SUBLANES

    @pl.when(pl.program_id(1) == 0)
    def _():
        uext_ref[0:halo, :] = jnp.zeros((halo, cc), F32)

    cg = cg_ref[0]
    uc = cg[:, 2 * cc:3 * cc] * cg[:, 0:cc]
    uext_ref[halo:halo + tm, :] = uc
    v = (cw_ref[0:1, :] * uext_ref[halo - 2:halo - 2 + tm, :]
         + cw_ref[1:2, :] * uext_ref[halo - 1:halo - 1 + tm, :]
         + cw_ref[2:3, :] * uc)
    ya = cg[:, cc:2 * cc] * v
    uext_ref[0:halo, :] = uc[tm - halo:tm, :]

    y2 = y_ref[0] + d_ref[...] * u_ref[0]
    z = _gelu_tanh(y2)
    gate = jax.nn.sigmoid(jnp.dot(z.astype(BF16), wglu_ref[...], preferred_element_type=F32)
                          + bglu_ref[...])
    yb = z * gate
    mix = (jnp.dot(ya.astype(BF16), wout_ref[0:cc, :], preferred_element_type=F32)
           + jnp.dot(yb.astype(BF16), wout_ref[cc:, :], preferred_element_type=F32))
    o_ref[0] = _layer_norm(DN_ALPHA * h_ref[0] + mix, g_ref[...], b_ref[...])


def _ab_mix(h, cg, u, y, conv_w, d_skip, w_glu, b_glu, w_out, g, b, batch, seq):
    d = h.shape[-1]
    tm = PROJ_ROWS

    def row(width):
        return pl.BlockSpec((1, tm, width), lambda bi, si: (bi, si, 0))

    r3 = lambda a: a.reshape(batch, seq, a.shape[-1])
    out = pl.pallas_call(
        _ab_mix_kernel,
        out_shape=jax.ShapeDtypeStruct((batch, seq, d), F32),
        grid=(batch, seq // tm),
        in_specs=[row(d), row(3 * CONV_CH), row(S5_CH), row(S5_CH),
                  _const_spec(conv_w.shape), _const_spec(d_skip.shape), _const_spec(w_glu.shape),
                  _const_spec(b_glu.shape), _const_spec(w_out.shape),
                  _const_spec(g.shape), _const_spec(b.shape)],
        out_specs=row(d),
        scratch_shapes=[pltpu.VMEM((tm + SUBLANES, CONV_CH), F32)],
        compiler_params=_cparams("parallel", "arbitrary"),
        name="ab_mix_out_ln",
    )(r3(h), r3(cg), r3(u), r3(y), conv_w, d_skip, w_glu, b_glu, w_out, g, b)
    return out.reshape(batch * seq, d)


def _rope_tables(positions, rot_dim, width, repeat):
    half = rot_dim // 2
    inv = ROPE_THETA ** (-jnp.arange(0, rot_dim, 2, dtype=F32) / rot_dim)
    ang = positions.astype(F32)[..., None] * inv
    cos, sin = jnp.cos(ang), jnp.sin(ang)
    lead = cos.shape[:-1]
    ones = jnp.ones(lead + (width - 2 * half,), F32)
    zeros = jnp.zeros(lead + (width - 2 * half,), F32)
    ct = jnp.concatenate([cos, cos, ones], axis=-1)
    st = jnp.concatenate([-sin, sin, zeros], axis=-1)
    return jnp.tile(ct, (1, 1, repeat)), jnp.tile(st, (1, 1, repeat))


def _rotate(t, cos_t, sin_t, half, period):
    lane = lax.broadcasted_iota(jnp.int32, t.shape, 1)
    first = (lane % period) < half
    partner = jnp.where(first, pltpu.roll(t, LANES - half, 1), pltpu.roll(t, half, 1))
    return t * cos_t + partner * sin_t


def _c_proj_kernel(x_ref, w_ref, ca_ref, sa_ref, ci_ref, si_ref,
                   q_ref, k_ref, vt_ref, qi_ref, ki_ref, wit_ref):
    y = jnp.dot(x_ref[...].astype(BF16), w_ref[...], preferred_element_type=F32)
    ca, sa, ci, si = ca_ref[...], sa_ref[...], ci_ref[...], si_ref[...]
    half_a = HEAD_DIM // ROT_FRAC // 2
    half_i = IDX_DIM // ROT_FRAC // 2
    nq = N_HEADS * HEAD_DIM
    nkv = N_KV_HEADS * HEAD_DIM
    nqi = IDX_HEADS * IDX_DIM
    for hd in range(N_HEADS):
        t = _rotate(y[:, hd * LANES:(hd + 1) * LANES], ca, sa, half_a, HEAD_DIM)
        q_ref[:, hd * LANES:(hd + 1) * LANES] = (t * QK_SCALE_LOG2).astype(BF16)
    for hd in range(N_KV_HEADS):
        t = y[:, nq + hd * LANES:nq + (hd + 1) * LANES]
        k_ref[:, hd * LANES:(hd + 1) * LANES] = _rotate(t, ca, sa, half_a, HEAD_DIM).astype(BF16)
    for hd in range(N_KV_HEADS):
        v = y[:, nq + nkv + hd * HEAD_DIM:nq + nkv + (hd + 1) * HEAD_DIM]
        vt_ref[0, hd * VT_ROWS:hd * VT_ROWS + HEAD_DIM, :] = v.T.astype(BF16)
        vt_ref[0, hd * VT_ROWS + HEAD_DIM:(hd + 1) * VT_ROWS, :] = jnp.ones(
            (BF16_ROWS, v.shape[0]), BF16)
    o = nq + 2 * nkv
    for pr in range(nqi // LANES):
        t = y[:, o + pr * LANES:o + (pr + 1) * LANES]
        qi_ref[:, pr * LANES:(pr + 1) * LANES] = _rotate(t, ci, si, half_i, IDX_DIM).astype(BF16)
    o += nqi
    t = y[:, o:o + LANES]
    lane = lax.broadcasted_iota(jnp.int32, t.shape, 1)
    ki_lo = jnp.where(lane < IDX_DIM, _rotate(t, ci, si, half_i, IDX_DIM), 0.0)
    ki_ref[:, 0:LANES] = ki_lo.astype(BF16)
    ki_ref[:, LANES:2 * LANES] = pltpu.roll(ki_lo, IDX_DIM, 1).astype(BF16)
    wit = t.T[IDX_DIM:IDX_DIM + IDX_HEADS, :]
    wit_ref[0] = wit * (IDX_HEADS ** -0.5 * IDX_DIM ** -0.5)


def _c_proj(x, w, ca, sa, ci, si, batch, seq):
    m, d = x.shape
    n = w.shape[1]
    tm = PROJ_ROWS
    spb = seq // tm
    nq = N_HEADS * HEAD_DIM
    nkv = N_KV_HEADS * HEAD_DIM
    nqi = IDX_HEADS * IDX_DIM
    row = lambda width: pl.BlockSpec((tm, width), lambda i: (i, 0))
    col = lambda height: pl.BlockSpec((1, height, tm), lambda i: (i // spb, 0, i % spb))
    return pl.pallas_call(
        _c_proj_kernel,
        out_shape=(jax.ShapeDtypeStruct((m, nq), BF16), jax.ShapeDtypeStruct((m, nkv), BF16),
                   jax.ShapeDtypeStruct((batch, N_KV_HEADS * VT_ROWS, seq), BF16),
                   jax.ShapeDtypeStruct((m, nqi), BF16),
                   jax.ShapeDtypeStruct((m, 2 * LANES), BF16),
                   jax.ShapeDtypeStruct((batch, IDX_HEADS, seq), F32)),
        grid=(m // tm,),
        in_specs=[row(d), _const_spec((d, n)), row(LANES), row(LANES), row(LANES), row(LANES)],
        out_specs=(row(nq), row(nkv), col(N_KV_HEADS * VT_ROWS), row(nqi), row(2 * LANES),
                   col(IDX_HEADS)),
        compiler_params=_cparams("parallel"),
        name="c_in_proj_rope",
    )(x, w, ca, sa, ci, si)


def _ordered_key(s):
    bits = pltpu.bitcast(s, jnp.int32)
    return jnp.where(bits < 0, -(bits ^ jnp.int32(-0x80000000)), bits)


def _unkey(k):
    return pltpu.bitcast(jnp.where(k < 0, (-k) ^ jnp.int32(-0x80000000), k), F32)


def _dsa_kernel(q_ref, qi_ref, wit_ref, k_ref, vt_ref, ki_ref, o_ref,
                key_ref, acc_ref, jcut_ref, sa_ref, sb_ref, *, topk):
    tq, tk, tqa = DSA_TQ, DSA_TK, DSA_TQA
    pos_bits = (k_ref.shape[1] - 1).bit_length()
    rep = N_HEADS // N_KV_HEADS
    t0 = pl.program_id(1) * tq
    nkb = (t0 + tq + tk - 1) // tk
    qpos = t0 + lax.broadcasted_iota(jnp.int32, (1, tq), 1)
    krow = lax.broadcasted_iota(jnp.int32, (tk, 1), 0)
    nt = (((1,), (1,)), ((), ()))

    qi = qi_ref[0]
    wit = wit_ref[0]

    def score_block(kb, carry):
        smax, smin = carry
        start = pl.multiple_of(kb * tk, tk)
        sc = jnp.zeros((tk, tq), F32)
        for hd in range(IDX_HEADS):
            pair, side = divmod(hd, LANES // IDX_DIM)
            kib = ki_ref[0, pl.ds(start, tk), side * LANES:(side + 1) * LANES]
            lg = lax.dot_general(kib, qi[:, pair * LANES:(pair + 1) * LANES], nt,
                                 preferred_element_type=F32)
            sc = sc + jnp.maximum(lg, 0.0) * wit[hd:hd + 1, :]
        causal = (start + krow) <= qpos
        key_ref[pl.ds(start, tk), :] = jnp.where(causal, _ordered_key(sc), KEY_NINF)
        smax = jnp.maximum(smax, jnp.max(jnp.where(causal, sc, -jnp.inf), axis=0, keepdims=True))
        smin = jnp.minimum(smin, jnp.min(jnp.where(causal, sc, jnp.inf), axis=0, keepdims=True))
        return smax, smin

    smax, smin = lax.fori_loop(0, nkb, score_block,
                               (jnp.full((1, tq), -jnp.inf, F32), jnp.full((1, tq), jnp.inf, F32)))

    def count(pred):
        def body(kb, c):
            start = pl.multiple_of(kb * tk, tk)
            m = jnp.where(pred(key_ref[pl.ds(start, tk), :], start), 1.0, 0.0)
            return c + jnp.sum(m.reshape(tk // COUNT_ROWS, COUNT_ROWS, tq), axis=0)
        c = lax.fori_loop(0, nkb, body, jnp.zeros((COUNT_ROWS, tq), F32))
        return jnp.sum(c, axis=0, keepdims=True).astype(jnp.int32)

    def any_query(mask):
        return jnp.max(jnp.where(mask, 1.0, 0.0)) > 0.5

    few = (qpos + 1) <= topk
    lo0 = jnp.where(few, KEY_NINF, _ordered_key(smin))
    hi0 = _ordered_key(smax) + 1
    done0 = jnp.where(few, 1, 0)

    def bisect(st, by_value):
        lo, hi, done = st
        if by_value:
            mid = _ordered_key(0.5 * _unkey(lo) + 0.5 * _unkey(jnp.minimum(hi, -KEY_NINF)))
            mid = jnp.minimum(jnp.maximum(mid, lo + 1), hi - 1)
        else:
            mid = (lo >> 1) + (hi >> 1) + (lo & hi & 1)
        c = count(lambda blk, start: blk >= mid)
        active = (done == 0) & (hi - lo != 1)
        ge = c >= topk
        lo = jnp.where(active & ge, mid, lo)
        hi = jnp.where(active & jnp.logical_not(ge), mid, hi)
        done = jnp.where(active & (c == topk), 1, done)
        return lo, hi, done

    def bis_cond(st):
        lo, hi, done = st
        return any_query((done == 0) & (hi - lo != 1))

    def bis_body(st):
        return bisect(bisect(bisect(st, True), True), False)

    thr, _, done = lax.while_loop(bis_cond, bis_body, (lo0, hi0, done0))

    tied = (done == 0) & (thr > KEY_NINF)
    jcut0 = jnp.where(thr > KEY_NINF, jnp.int32(0x7FFFFFFF), jnp.int32(-1))

    def resolve_ties():
        need = topk - count(lambda blk, start: blk > thr)

        def step(b, ans):
            cand = ans | (jnp.int32(1) << (pos_bits - 1 - b))
            c = count(lambda blk, start: (blk == thr) & ((start + krow) < cand))
            return jnp.where(c < need, cand, ans)

        ans = lax.fori_loop(0, pos_bits, step, jnp.zeros((1, tq), jnp.int32))
        jcut_ref[...] = jnp.where(tied, ans, jcut0)

    jcut_ref[...] = jcut0
    pl.when(any_query(tied))(resolve_ties)
    jcut = jcut_ref[...]

    for u in range(tq // tqa):
        qs = slice(u * tqa, (u + 1) * tqa)
        thr_u, jcut_u = thr[:, qs], jcut[:, qs]
        nkb_u = (t0 + (u + 1) * tqa + tk - 1) // tk
        qg = [jnp.concatenate([q_ref[0, qs, (g * rep + r) * HEAD_DIM:(g * rep + r + 1) * HEAD_DIM]
                               for r in range(rep)], axis=0) for g in range(N_KV_HEADS)]
        acc_ref[...] = jnp.zeros_like(acc_ref)
        ncol = rep * tqa

        def col_reduce(op, x):
            part = op(x.reshape(tk // COUNT_ROWS, COUNT_ROWS, ncol), axis=0)
            return op(part, axis=0, keepdims=True)

        def score_into(s_ref, kb):
            start = pl.multiple_of(kb * tk, tk)
            key = key_ref[pl.ds(start, tk), qs]
            sel = (((key > thr_u) | ((key == thr_u) & ((start + krow) <= jcut_u)))
                   & (key > KEY_NINF))
            sel = jnp.concatenate([sel] * rep, axis=1)
            cmax = []
            for g in range(N_KV_HEADS):
                kblk = k_ref[0, pl.ds(start, tk), g * HEAD_DIM:(g + 1) * HEAD_DIM]
                s = lax.dot_general(kblk, qg[g], nt, preferred_element_type=F32)
                s = jnp.where(sel, s, -jnp.inf)
                s_ref[g] = s
                cmax.append(col_reduce(jnp.max, s))
            return cmax

        def accumulate(s_ref, kb, g, m_prev, m_cur):
            start = pl.multiple_of(kb * tk, tk)
            vtb = vt_ref[0, g * VT_ROWS:(g + 1) * VT_ROWS, pl.ds(start, tk)]
            m_safe = jnp.where(m_cur == -jnp.inf, 0.0, m_cur)
            p = jnp.exp2(s_ref[g] - m_safe)
            alpha = jnp.exp2(m_prev - m_safe)
            acc_ref[g] = alpha * acc_ref[g] + jnp.dot(vtb, p.astype(BF16),
                                                      preferred_element_type=F32)

        def step(s_cur, s_next, kb, st):
            cmax = score_into(s_next, kb + 1)
            new = []
            for g in range(N_KV_HEADS):
                m_prev, m_cur = st[2 * g:2 * g + 2]
                accumulate(s_cur, kb, g, m_prev, m_cur)
                new += [m_cur, jnp.maximum(m_cur, cmax[g])]
            return tuple(new)

        def finish(s_cur, kb, st):
            for g in range(N_KV_HEADS):
                accumulate(s_cur, kb, g, st[2 * g], st[2 * g + 1])

        cmax0 = score_into(sa_ref, 0)
        st0 = ()
        for g in range(N_KV_HEADS):
            st0 += (jnp.full((1, ncol), -jnp.inf, F32), cmax0[g])

        def pair(j, st):
            return step(sb_ref, sa_ref, 2 * j + 1, step(sa_ref, sb_ref, 2 * j, st))

        st = lax.fori_loop(0, (nkb_u - 1) // 2, pair, st0)
        last = nkb_u - 1

        @pl.when(last % 2 == 0)
        def _():
            finish(sa_ref, last, st)

        @pl.when(last % 2 == 1)
        def _():
            finish(sb_ref, last, step(sa_ref, sb_ref, last - 1, st))

        for g in range(N_KV_HEADS):
            acc = acc_ref[g]
            og = acc[:HEAD_DIM] / acc[HEAD_DIM:HEAD_DIM + 1]
            for r in range(rep):
                hd = g * rep + r
                o_ref[0, qs, hd * HEAD_DIM:(hd + 1) * HEAD_DIM] = (
                    og[:, r * tqa:(r + 1) * tqa].T.astype(BF16))


def _dsa(q, k, vt, qi, ki, wit, batch, seq):
    topk = min(TOPK_MAX, seq // 4)
    tq = DSA_TQ
    rep = N_HEADS // N_KV_HEADS
    r3 = lambda a: a.reshape(batch, seq, a.shape[-1])
    qrow = lambda width: pl.BlockSpec((1, tq, width), lambda b, i: (b, i, 0))
    full = lambda width: pl.BlockSpec((1, seq, width), lambda b, i: (b, 0, 0))
    out = pl.pallas_call(
        functools.partial(_dsa_kernel, topk=topk),
        out_shape=jax.ShapeDtypeStruct((batch, seq, N_HEADS * HEAD_DIM), BF16),
        grid=(batch, seq // tq),
        in_specs=[qrow(q.shape[-1]), qrow(qi.shape[-1]),
                  pl.BlockSpec((1, wit.shape[1], tq), lambda b, i: (b, 0, i)),
                  full(k.shape[-1]),
                  pl.BlockSpec((1, vt.shape[1], seq), lambda b, i: (b, 0, 0)),
                  full(ki.shape[-1])],
        out_specs=qrow(N_HEADS * HEAD_DIM),
        scratch_shapes=[pltpu.VMEM((seq, tq), jnp.int32),
                        pltpu.VMEM((N_KV_HEADS, VT_ROWS, rep * DSA_TQA), F32),
                        pltpu.VMEM((1, tq), jnp.int32),
                        pltpu.VMEM((N_KV_HEADS, DSA_TK, rep * DSA_TQA), F32),
                        pltpu.VMEM((N_KV_HEADS, DSA_TK, rep * DSA_TQA), F32)],
        compiler_params=_cparams("parallel", "arbitrary"),
        name="dsa_attention",
    )(r3(q), r3(qi), wit, r3(k), vt, r3(ki))
    return out.reshape(batch * seq, N_HEADS * HEAD_DIM)


def _out_ln_kernel(h_ref, a_ref, w_ref, g_ref, b_ref, o_ref):
    mix = jnp.dot(a_ref[...], w_ref[...], preferred_element_type=F32)
    o_ref[...] = _layer_norm(DN_ALPHA * h_ref[...] + mix, g_ref[...], b_ref[...])


def _out_ln(h, a, w, g, b):
    m, d = h.shape
    tm = PROJ_ROWS
    return pl.pallas_call(
        _out_ln_kernel,
        out_shape=jax.ShapeDtypeStruct((m, d), F32),
        grid=(m // tm,),
        in_specs=[pl.BlockSpec((tm, d), lambda i: (i, 0)),
                  pl.BlockSpec((tm, a.shape[1]), lambda i: (i, 0)),
                  _const_spec(w.shape), _const_spec(g.shape), _const_spec(b.shape)],
        out_specs=pl.BlockSpec((tm, d), lambda i: (i, 0)),
        compiler_params=_cparams("parallel"),
        name="c_out_proj_ln",
    )(h, a, w, g, b)


def kernel(x, p, positions, ln_g, ln_b, ffn_w1, ffn_w3, ffn_w2, ple_w_proj, ple_w_gate,
           ab_w_in, ab_w_out, conv_w, s5_lam_re, s5_lam_im, s5_log_dt, s5_b_re, s5_b_im,
           s5_c_re, s5_c_im, s5_d, s5_w_glu, s5_b_glu, c_w_in, c_w_out):
    batch, seq, d = x.shape
    m = batch * seq
    depth = ffn_w1.shape[0]
    g3 = ln_g.reshape(depth, 3, 1, d)
    b3 = ln_b.reshape(depth, 3, 1, d)
    wp = ple_w_proj.astype(BF16)
    wg = ple_w_gate.astype(BF16)
    p2 = p.reshape(depth, m, p.shape[-1])

    pos = positions
    ca, sa = _rope_tables(pos, HEAD_DIM // ROT_FRAC, HEAD_DIM, 1)
    ci, si = _rope_tables(pos, IDX_DIM // ROT_FRAC, IDX_DIM, LANES // IDX_DIM)
    ca, sa, ci, si = (t.reshape(m, LANES) for t in (ca, sa, ci, si))
    c_in = c_w_in.shape[-1]
    c_pad = -c_in % LANES
    cw_in = jnp.pad(c_w_in, ((0, 0), (0, 0), (0, c_pad))).astype(BF16)

    h = x.reshape(m, d)
    for i in range(depth):
        h = _ffn_ln(h, ffn_w1, ffn_w3, ffn_w2, i, 0, g3[i, 0], b3[i, 0])
        j = i // 2
        if i % 2 == 0:
            cg, u = _ab_proj(h, ab_w_in[j].astype(BF16))
            tables = _s5_tables(s5_lam_re[j], s5_lam_im[j], s5_log_dt[j], s5_b_re[j], s5_b_im[j],
                                s5_c_re[j], s5_c_im[j])
            y = _s5_linear(u, tables, batch, seq)
            h = _ab_mix(h, cg, u, y, conv_w[j], s5_d[j].reshape(1, -1), s5_w_glu[j].astype(BF16),
                        s5_b_glu[j].reshape(1, -1), ab_w_out[j].astype(BF16), g3[i, 1], b3[i, 1],
                        batch, seq)
        else:
            q, k, vt, qi, ki, wit = _c_proj(h, cw_in[j], ca, sa, ci, si, batch, seq)
            a = _dsa(q, k, vt, qi, ki, wit, batch, seq)
            h = _out_ln(h, a, c_w_out[j].astype(BF16), g3[i, 1], b3[i, 1])
        h = _ffn_ln(h, ffn_w1, ffn_w3, ffn_w2, i, 1, g3[i, 2], b3[i, 2],
                    ple=(p2[i], wp[i], wg[i]))
    return h.reshape(batch, seq, d)
```

```python
import functools
import math

import jax
import jax.numpy as jnp
from jax import lax
from jax.experimental import pallas as pl
from jax.experimental.pallas import tpu as pltpu

F32 = jnp.float32
BF16 = jnp.bfloat16
HIGHEST = lax.Precision.HIGHEST

DEPTH = 4
LN_EPS = 1e-5
DN_ALPHA = (2 * DEPTH) ** 0.25
CONV_WIDTH = 3
CONV_CH = 512
S5_CH = 512
S5_GROUP = 16
S5_GROUPS = S5_CH // S5_GROUP
S5_STATE = 64
N_HEADS = 8
HEAD_DIM = 128
N_KV_HEADS = 2
IDX_HEADS = 8
IDX_DIM = 64
TOPK_MAX = 256
ROPE_THETA = 500000.0
ROT_FRAC = 4

LANES = 128
SUBLANES = 8
VMEM_LIMIT = 60 * 1024 * 1024

FFN_ROWS = 512
FF_CHUNK = 256
PROJ_ROWS = 512
S5_T = 16
DSA_TQ = 256
DSA_TQA = 128
DSA_TK = 512
COUNT_ROWS = 64
BF16_ROWS = 16
VT_ROWS = HEAD_DIM + BF16_ROWS
QK_SCALE_LOG2 = HEAD_DIM ** -0.5 * math.log2(math.e)
KEY_NINF = -0x7F800000


def _cparams(*sem):
    return pltpu.CompilerParams(dimension_semantics=sem, vmem_limit_bytes=VMEM_LIMIT)


def _layer_norm(y, g, b):
    mu = jnp.mean(y, axis=-1, keepdims=True)
    d = y - mu
    var = jnp.mean(d * d, axis=-1, keepdims=True)
    return d * lax.rsqrt(var + LN_EPS) * g + b


def _const_spec(shape):
    nd = len(shape)
    return pl.BlockSpec(shape, lambda *_: (0,) * nd, pipeline_mode=pl.Buffered(1))


def _ffn_body(x_ref, w1_ref, w3_ref, w2_ref, g_ref, b_ref, acc_ref):
    x = x_ref[...]
    xb = x.astype(BF16)
    acc_ref[...] = jnp.zeros_like(acc_ref)
    for c in range(w1_ref.shape[1] // FF_CHUNK):
        cols = slice(c * FF_CHUNK, (c + 1) * FF_CHUNK)
        a = jnp.dot(xb, w1_ref[:, cols].astype(BF16), preferred_element_type=F32)
        gt = jnp.dot(xb, w3_ref[:, cols].astype(BF16), preferred_element_type=F32)
        act = (a * jax.nn.sigmoid(a)) * gt
        acc_ref[...] += jnp.dot(act.astype(BF16), w2_ref[cols, :].astype(BF16),
                                preferred_element_type=F32)
    y = DN_ALPHA * x + 0.5 * acc_ref[...]
    return _layer_norm(y, g_ref[...], b_ref[...])


def _ffn_kernel(x_ref, w1_ref, w3_ref, w2_ref, g_ref, b_ref, o_ref, acc_ref):
    o_ref[...] = _ffn_body(x_ref, w1_ref, w3_ref, w2_ref, g_ref, b_ref, acc_ref)


def _ffn_ple_kernel(x_ref, w1_ref, w3_ref, w2_ref, g_ref, b_ref, p_ref, wp_ref, wg_ref,
                    o_ref, acc_ref):
    h = _ffn_body(x_ref, w1_ref, w3_ref, w2_ref, g_ref, b_ref, acc_ref)
    gate = jax.nn.sigmoid(jnp.dot(h.astype(BF16), wg_ref[...], preferred_element_type=F32))
    pe = jnp.dot(p_ref[...].astype(BF16), wp_ref[...], preferred_element_type=F32)
    o_ref[...] = h + pe * gate


def _ffn_ln(x, w1, w3, w2, layer, half, g, b, ple=None):
    m, d = x.shape
    row = pl.BlockSpec((FFN_ROWS, d), lambda i: (i, 0))

    def whole(w):
        return pl.BlockSpec((None, None) + w.shape[2:], lambda i: (layer, half, 0, 0),
                            pipeline_mode=pl.Buffered(1))

    in_specs = [row, whole(w1), whole(w3), whole(w2), _const_spec((1, d)), _const_spec((1, d))]
    args = [x, w1, w3, w2, g, b]
    body = _ffn_kernel
    name = "ffn_ln"
    if ple is not None:
        p, wp, wg = ple
        in_specs += [pl.BlockSpec((FFN_ROWS, p.shape[1]), lambda i: (i, 0)),
                     _const_spec(wp.shape), _const_spec(wg.shape)]
        args += [p, wp, wg]
        body = _ffn_ple_kernel
        name = "ffn_ln_ple"
    return pl.pallas_call(
        body,
        out_shape=jax.ShapeDtypeStruct((m, d), F32),
        grid=(m // FFN_ROWS,),
        in_specs=in_specs,
        out_specs=row,
        scratch_shapes=[pltpu.VMEM((FFN_ROWS, d), F32)],
        compiler_params=_cparams("parallel"),
        name=name,
    )(*args)


def _ab_proj_kernel(x_ref, w_ref, cg_ref, u_ref):
    y = jnp.dot(x_ref[...].astype(BF16), w_ref[...], preferred_element_type=F32)
    n_cg = cg_ref.shape[1]
    cg_ref[...] = y[:, :n_cg]
    u_ref[...] = y[:, n_cg:]


def _ab_proj(x, w):
    m, d = x.shape
    n = w.shape[1]
    n_cg = 3 * CONV_CH
    return pl.pallas_call(
        _ab_proj_kernel,
        out_shape=(jax.ShapeDtypeStruct((m, n_cg), F32), jax.ShapeDtypeStruct((m, n - n_cg), F32)),
        grid=(m // PROJ_ROWS,),
        in_specs=[pl.BlockSpec((PROJ_ROWS, d), lambda i: (i, 0)), _const_spec((d, n))],
        out_specs=(pl.BlockSpec((PROJ_ROWS, n_cg), lambda i: (i, 0)),
                   pl.BlockSpec((PROJ_ROWS, n - n_cg), lambda i: (i, 0))),
        compiler_params=_cparams("parallel"),
        name="ab_in_proj",
    )(x, w)


def _s5_tables(lam_re, lam_im, log_dt, b_re, b_im, c_re, c_im):
    lr = jnp.minimum(lam_re, -1e-4)
    li = lam_im
    dt = jnp.exp(log_dt)[:, None]
    mag = jnp.exp(lr * dt)
    ab_re = mag * jnp.cos(li * dt)
    ab_im = mag * jnp.sin(li * dt)
    nr, ni = ab_re - 1.0, ab_im
    den = lr * lr + li * li
    f_re = (nr * lr + ni * li) / den
    f_im = (ni * lr - nr * li) / den
    bb_re = f_re[..., None] * b_re - f_im[..., None] * b_im
    bb_im = f_re[..., None] * b_im + f_im[..., None] * b_re
    pw_re = [jnp.ones_like(ab_re)]
    pw_im = [jnp.zeros_like(ab_im)]
    for _ in range(S5_T):
        pr, pi = pw_re[-1], pw_im[-1]
        pw_re.append(pr * ab_re - pi * ab_im)
        pw_im.append(pr * ab_im + pi * ab_re)
    pw_re = jnp.stack(pw_re)
    pw_im = jnp.stack(pw_im)
    cp_re = c_re[None] * pw_re[:, :, None, :] - c_im[None] * pw_im[:, :, None, :]
    cp_im = c_re[None] * pw_im[:, :, None, :] + c_im[None] * pw_re[:, :, None, :]
    taps = (jnp.einsum('ngcp,gpd->ngcd', cp_re[:S5_T], bb_re, precision=HIGHEST)
            - jnp.einsum('ngcp,gpd->ngcd', cp_im[:S5_T], bb_im, precision=HIGHEST))
    s_idx = jnp.arange(S5_T)[:, None]
    t_idx = jnp.arange(S5_T)[None, :]
    lag = t_idx - s_idx
    k_st = taps[jnp.clip(lag, 0, S5_T - 1)]
    k_st = jnp.where((lag >= 0)[:, :, None, None, None], k_st, 0.0)
    toep = jnp.transpose(k_st, (2, 0, 4, 1, 3)).reshape(S5_GROUPS, S5_T * S5_GROUP, S5_T * S5_GROUP)
    rev_re = pw_re[S5_T - 1::-1][:S5_T]
    rev_im = pw_im[S5_T - 1::-1][:S5_T]
    win_re = rev_re[..., None] * bb_re[None] - rev_im[..., None] * bb_im[None]
    win_im = rev_re[..., None] * bb_im[None] + rev_im[..., None] * bb_re[None]
    w_in = jnp.concatenate([win_re, win_im], axis=2)
    w_in = jnp.transpose(w_in, (1, 0, 3, 2)).reshape(S5_GROUPS, S5_T * S5_GROUP, 2 * S5_STATE)
    wo_re = jnp.transpose(cp_re[1:], (1, 3, 0, 2))
    wo_im = jnp.transpose(-cp_im[1:], (1, 3, 0, 2))
    w_out = jnp.concatenate([wo_re, wo_im], axis=1).reshape(S5_GROUPS, 2 * S5_STATE, S5_T * S5_GROUP)
    a1 = jnp.concatenate([pw_re[S5_T], pw_re[S5_T]], axis=-1)
    a2 = jnp.concatenate([-pw_im[S5_T], pw_im[S5_T]], axis=-1)
    return toep, w_in, w_out, a1, a2


def _s5_local_kernel(uc_ref, win_ref, o_ref):
    o_ref[...] = jnp.dot(uc_ref[...], win_ref[0], preferred_element_type=F32, precision=HIGHEST)


def _s5_local(uc, w_in):
    rows = uc.shape[0]
    g, width, ns = w_in.shape
    return pl.pallas_call(
        _s5_local_kernel,
        out_shape=jax.ShapeDtypeStruct((rows, g * ns), F32),
        grid=(g,),
        in_specs=[pl.BlockSpec((rows, width), lambda i: (0, i)),
                  pl.BlockSpec((1, width, ns), lambda i: (i, 0, 0))],
        out_specs=pl.BlockSpec((rows, ns), lambda i: (0, i)),
        compiler_params=_cparams("parallel"),
        name="s5_chunk_state_in",
    )(uc, w_in)


def _s5_scan_kernel(g_ref, a1_ref, a2_ref, o_ref):
    a1 = a1_ref[...]
    a2 = a2_ref[...]
    half = a1.shape[1] // 2

    a2s = pltpu.roll(a2, half, 1)

    def step(j, st):
        x, xs = st
        o_ref[j] = x
        g = g_ref[j]
        return a1 * x + a2 * xs + g, a1 * xs + a2s * x + pltpu.roll(g, half, 1)

    zero = jnp.zeros(a1.shape, F32)
    lax.fori_loop(0, g_ref.shape[0], step, (zero, zero), unroll=8)


def _s5_scan(gl, a1, a2):
    nch, rows, ns = gl.shape
    rb = 32
    return pl.pallas_call(
        _s5_scan_kernel,
        out_shape=jax.ShapeDtypeStruct((nch, rows, ns), F32),
        grid=(rows // rb,),
        in_specs=[pl.BlockSpec((nch, rb, ns), lambda i: (0, i, 0)),
                  pl.BlockSpec((rb, ns), lambda i: (i, 0)),
                  pl.BlockSpec((rb, ns), lambda i: (i, 0))],
        out_specs=pl.BlockSpec((nch, rb, ns), lambda i: (0, i, 0)),
        compiler_params=_cparams("parallel"),
        name="s5_chunk_scan",
    )(gl, a1, a2)


def _s5_out_kernel(uc_ref, xp_ref, toep_ref, wout_ref, o_ref):
    y = jnp.dot(uc_ref[...], toep_ref[0], preferred_element_type=F32, precision=HIGHEST)
    y = y + jnp.dot(xp_ref[...], wout_ref[0], preferred_element_type=F32, precision=HIGHEST)
    o_ref[...] = y


def _s5_out(uc, xprev, toep, w_out):
    rows = uc.shape[0]
    g, ns, width = w_out.shape
    return pl.pallas_call(
        _s5_out_kernel,
        out_shape=jax.ShapeDtypeStruct((rows, g * width), F32),
        grid=(g,),
        in_specs=[pl.BlockSpec((rows, width), lambda i: (0, i)),
                  pl.BlockSpec((rows, ns), lambda i: (0, i)),
                  pl.BlockSpec((1, width, width), lambda i: (i, 0, 0)),
                  pl.BlockSpec((1, ns, width), lambda i: (i, 0, 0))],
        out_specs=pl.BlockSpec((rows, width), lambda i: (0, i)),
        compiler_params=_cparams("parallel"),
        name="s5_chunk_out",
    )(uc, xprev, toep, w_out)


def _to_chunks_kernel(u_ref, o_ref):
    nch = o_ref.shape[0]
    width = S5_T * S5_GROUP
    u3 = u_ref[...].reshape(nch, S5_T, u_ref.shape[1])
    steps = [u3[:, t, :] for t in range(S5_T)]
    for g in range(S5_GROUPS):
        o_ref[:, g * width:(g + 1) * width] = jnp.concatenate(
            [s[:, g * S5_GROUP:(g + 1) * S5_GROUP] for s in steps], axis=1)


def _from_chunks_kernel(yc_ref, o_ref):
    nch = yc_ref.shape[0]
    width = S5_T * S5_GROUP
    yc = yc_ref[...]
    rows = [jnp.concatenate([yc[:, g * width + t * S5_GROUP:g * width + (t + 1) * S5_GROUP]
                             for g in range(S5_GROUPS)], axis=1) for t in range(S5_T)]
    o_ref[...] = jnp.stack(rows, axis=1).reshape(nch * S5_T, o_ref.shape[1])


def _to_chunks(u):
    m, ch = u.shape
    tm = PROJ_ROWS
    return pl.pallas_call(
        _to_chunks_kernel,
        out_shape=jax.ShapeDtypeStruct((m // S5_T, S5_T * ch), F32),
        grid=(m // tm,),
        in_specs=[pl.BlockSpec((tm, ch), lambda i: (i, 0))],
        out_specs=pl.BlockSpec((tm // S5_T, S5_T * ch), lambda i: (i, 0)),
        compiler_params=_cparams("parallel"),
        name="s5_to_chunks",
    )(u)


def _from_chunks(yc):
    rows, width = yc.shape
    tm = PROJ_ROWS
    return pl.pallas_call(
        _from_chunks_kernel,
        out_shape=jax.ShapeDtypeStruct((rows * S5_T, width // S5_T), F32),
        grid=(rows * S5_T // tm,),
        in_specs=[pl.BlockSpec((tm // S5_T, width), lambda i: (i, 0))],
        out_specs=pl.BlockSpec((tm, width // S5_T), lambda i: (i, 0)),
        compiler_params=_cparams("parallel"),
        name="s5_from_chunks",
    )(yc)


def _s5_linear(u, tables, batch, seq):
    toep, w_in, w_out, a1, a2 = tables
    nch = seq // S5_T
    ns = w_in.shape[2]
    uc = _to_chunks(u)
    gl = _s5_local(uc, w_in)
    gl = jnp.transpose(gl.reshape(batch, nch, S5_GROUPS * ns), (1, 0, 2))
    gl = gl.reshape(nch, batch * S5_GROUPS, ns)
    xprev = _s5_scan(gl, jnp.tile(a1, (batch, 1)), jnp.tile(a2, (batch, 1)))
    xprev = jnp.transpose(xprev.reshape(nch, batch, S5_GROUPS * ns), (1, 0, 2))
    xprev = xprev.reshape(batch * nch, S5_GROUPS * ns)
    yc = _s5_out(uc, xprev, toep, w_out)
    return _from_chunks(yc)


def _gelu_tanh(x):
    c = math.sqrt(2.0 / math.pi)
    return 0.5 * x * (1.0 + jnp.tanh(c * (x + 0.044715 * (x * x * x))))


def _ab_mix_kernel(h_ref, cg_ref, u_ref, y_ref, cw_ref, d_ref, wglu_ref, bglu_ref, wout_ref,
                   g_ref, b_ref, o_ref, uext_ref):
    tm = h_ref.shape[1]
    cc = CONV_CH
    halo = SUBLANES

    @pl.when(pl.program_id(1) == 0)
    def _():
        uext_ref[0:halo, :] = jnp.zeros((halo, cc), F32)

    cg = cg_ref[0]
    uc = cg[:, 2 * cc:3 * cc] * cg[:, 0:cc]
    uext_ref[halo:halo + tm, :] = uc
    v = (cw_ref[0:1, :] * uext_ref[halo - 2:halo - 2 + tm, :]
         + cw_ref[1:2, :] * uext_ref[halo - 1:halo - 1 + tm, :]
         + cw_ref[2:3, :] * uc)
    ya = cg[:, cc:2 * cc] * v
    uext_ref[0:halo, :] = uc[tm - halo:tm, :]

    y2 = y_ref[0] + d_ref[...] * u_ref[0]
    z = _gelu_tanh(y2)
    gate = jax.nn.sigmoid(jnp.dot(z.astype(BF16), wglu_ref[...], preferred_element_type=F32)
                          + bglu_ref[...])
    yb = z * gate
    mix = (jnp.dot(ya.astype(BF16), wout_ref[0:cc, :], preferred_element_type=F32)
           + jnp.dot(yb.astype(BF16), wout_ref[cc:, :], preferred_element_type=F32))
    o_ref[0] = _layer_norm(DN_ALPHA * h_ref[0] + mix, g_ref[...], b_ref[...])


def _ab_mix(h, cg, u, y, conv_w, d_skip, w_glu, b_glu, w_out, g, b, batch, seq):
    d = h.shape[-1]
    tm = PROJ_ROWS

    def row(width):
        return pl.BlockSpec((1, tm, width), lambda bi, si: (bi, si, 0))

    r3 = lambda a: a.reshape(batch, seq, a.shape[-1])
    out = pl.pallas_call(
        _ab_mix_kernel,
        out_shape=jax.ShapeDtypeStruct((batch, seq, d), F32),
        grid=(batch, seq // tm),
        in_specs=[row(d), row(3 * CONV_CH), row(S5_CH), row(S5_CH),
                  _const_spec(conv_w.shape), _const_spec(d_skip.shape), _const_spec(w_glu.shape),
                  _const_spec(b_glu.shape), _const_spec(w_out.shape),
                  _const_spec(g.shape), _const_spec(b.shape)],
        out_specs=row(d),
        scratch_shapes=[pltpu.VMEM((tm + SUBLANES, CONV_CH), F32)],
        compiler_params=_cparams("parallel", "arbitrary"),
        name="ab_mix_out_ln",
    )(r3(h), r3(cg), r3(u), r3(y), conv_w, d_skip, w_glu, b_glu, w_out, g, b)
    return out.reshape(batch * seq, d)


def _rope_tables(positions, rot_dim, width, repeat):
    half = rot_dim // 2
    inv = ROPE_THETA ** (-jnp.arange(0, rot_dim, 2, dtype=F32) / rot_dim)
    ang = positions.astype(F32)[..., None] * inv
    cos, sin = jnp.cos(ang), jnp.sin(ang)
    lead = cos.shape[:-1]
    ones = jnp.ones(lead + (width - 2 * half,), F32)
    zeros = jnp.zeros(lead + (width - 2 * half,), F32)
    ct = jnp.concatenate([cos, cos, ones], axis=-1)
    st = jnp.concatenate([-sin, sin, zeros], axis=-1)
    return jnp.tile(ct, (1, 1, repeat)), jnp.tile(st, (1, 1, repeat))


def _rotate(t, cos_t, sin_t, half, period):
    lane = lax.broadcasted_iota(jnp.int32, t.shape, 1)
    first = (lane % period) < half
    partner = jnp.where(first, pltpu.roll(t, LANES - half, 1), pltpu.roll(t, half, 1))
    return t * cos_t + partner * sin_t


def _c_proj_kernel(x_ref, w_ref, ca_ref, sa_ref, ci_ref, si_ref,
                   q_ref, k_ref, vt_ref, qi_ref, ki_ref, wit_ref):
    y = jnp.dot(x_ref[...].astype(BF16), w_ref[...], preferred_element_type=F32)
    ca, sa, ci, si = ca_ref[...], sa_ref[...], ci_ref[...], si_ref[...]
    half_a = HEAD_DIM // ROT_FRAC // 2
    half_i = IDX_DIM // ROT_FRAC // 2
    nq = N_HEADS * HEAD_DIM
    nkv = N_KV_HEADS * HEAD_DIM
    nqi = IDX_HEADS * IDX_DIM
    for hd in range(N_HEADS):
        t = _rotate(y[:, hd * LANES:(hd + 1) * LANES], ca, sa, half_a, HEAD_DIM)
        q_ref[:, hd * LANES:(hd + 1) * LANES] = (t * QK_SCALE_LOG2).astype(BF16)
    for hd in range(N_KV_HEADS):
        t = y[:, nq + hd * LANES:nq + (hd + 1) * LANES]
        k_ref[:, hd * LANES:(hd + 1) * LANES] = _rotate(t, ca, sa, half_a, HEAD_DIM).astype(BF16)
    for hd in range(N_KV_HEADS):
        v = y[:, nq + nkv + hd * HEAD_DIM:nq + nkv + (hd + 1) * HEAD_DIM]
        vt_ref[0, hd * VT_ROWS:hd * VT_ROWS + HEAD_DIM, :] = v.T.astype(BF16)
        vt_ref[0, hd * VT_ROWS + HEAD_DIM:(hd + 1) * VT_ROWS, :] = jnp.ones(
            (BF16_ROWS, v.shape[0]), BF16)
    o = nq + 2 * nkv
    for pr in range(nqi // LANES):
        t = y[:, o + pr * LANES:o + (pr + 1) * LANES]
        qi_ref[:, pr * LANES:(pr + 1) * LANES] = _rotate(t, ci, si, half_i, IDX_DIM).astype(BF16)
    o += nqi
    t = y[:, o:o + LANES]
    lane = lax.broadcasted_iota(jnp.int32, t.shape, 1)
    ki_lo = jnp.where(lane < IDX_DIM, _rotate(t, ci, si, half_i, IDX_DIM), 0.0)
    ki_ref[:, 0:LANES] = ki_lo.astype(BF16)
    ki_ref[:, LANES:2 * LANES] = pltpu.roll(ki_lo, IDX_DIM, 1).astype(BF16)
    wit = t.T[IDX_DIM:IDX_DIM + IDX_HEADS, :]
    wit_ref[0] = wit * (IDX_HEADS ** -0.5 * IDX_DIM ** -0.5)


def _c_proj(x, w, ca, sa, ci, si, batch, seq):
    m, d = x.shape
    n = w.shape[1]
    tm = PROJ_ROWS
    spb = seq // tm
    nq = N_HEADS * HEAD_DIM
    nkv = N_KV_HEADS * HEAD_DIM
    nqi = IDX_HEADS * IDX_DIM
    row = lambda width: pl.BlockSpec((tm, width), lambda i: (i, 0))
    col = lambda height: pl.BlockSpec((1, height, tm), lambda i: (i // spb, 0, i % spb))
    return pl.pallas_call(
        _c_proj_kernel,
        out_shape=(jax.ShapeDtypeStruct((m, nq), BF16), jax.ShapeDtypeStruct((m, nkv), BF16),
                   jax.ShapeDtypeStruct((batch, N_KV_HEADS * VT_ROWS, seq), BF16),
                   jax.ShapeDtypeStruct((m, nqi), BF16),
                   jax.ShapeDtypeStruct((m, 2 * LANES), BF16),
                   jax.ShapeDtypeStruct((batch, IDX_HEADS, seq), F32)),
        grid=(m // tm,),
        in_specs=[row(d), _const_spec((d, n)), row(LANES), row(LANES), row(LANES), row(LANES)],
        out_specs=(row(nq), row(nkv), col(N_KV_HEADS * VT_ROWS), row(nqi), row(2 * LANES),
                   col(IDX_HEADS)),
        compiler_params=_cparams("parallel"),
        name="c_in_proj_rope",
    )(x, w, ca, sa, ci, si)


def _ordered_key(s):
    bits = pltpu.bitcast(s, jnp.int32)
    return jnp.where(bits < 0, -(bits ^ jnp.int32(-0x80000000)), bits)


def _unkey(k):
    return pltpu.bitcast(jnp.where(k < 0, (-k) ^ jnp.int32(-0x80000000), k), F32)


def _dsa_kernel(q_ref, qi_ref, wit_ref, k_ref, vt_ref, ki_ref, o_ref,
                key_ref, acc_ref, jcut_ref, sa_ref, sb_ref, *, topk):
    tq, tk, tqa = DSA_TQ, DSA_TK, DSA_TQA
    pos_bits = (k_ref.shape[1] - 1).bit_length()
    rep = N_HEADS // N_KV_HEADS
    t0 = pl.program_id(1) * tq
    nkb = (t0 + tq + tk - 1) // tk
    qpos = t0 + lax.broadcasted_iota(jnp.int32, (1, tq), 1)
    krow = lax.broadcasted_iota(jnp.int32, (tk, 1), 0)
    nt = (((1,), (1,)), ((), ()))

    qi = qi_ref[0]
    wit = wit_ref[0]

    def score_block(kb, carry):
        smax, smin = carry
        start = pl.multiple_of(kb * tk, tk)
        sc = jnp.zeros((tk, tq), F32)
        for hd in range(IDX_HEADS):
            pair, side = divmod(hd, LANES // IDX_DIM)
            kib = ki_ref[0, pl.ds(start, tk), side * LANES:(side + 1) * LANES]
            lg = lax.dot_general(kib, qi[:, pair * LANES:(pair + 1) * LANES], nt,
                                 preferred_element_type=F32)
            sc = sc + jnp.maximum(lg, 0.0) * wit[hd:hd + 1, :]
        causal = (start + krow) <= qpos
        key_ref[pl.ds(start, tk), :] = jnp.where(causal, _ordered_key(sc), KEY_NINF)
        smax = jnp.maximum(smax, jnp.max(jnp.where(causal, sc, -jnp.inf), axis=0, keepdims=True))
        smin = jnp.minimum(smin, jnp.min(jnp.where(causal, sc, jnp.inf), axis=0, keepdims=True))
        return smax, smin

    smax, smin = lax.fori_loop(0, nkb, score_block,
                               (jnp.full((1, tq), -jnp.inf, F32), jnp.full((1, tq), jnp.inf, F32)))

    def count(pred):
        def body(kb, c):
            start = pl.multiple_of(kb * tk, tk)
            m = jnp.where(pred(key_ref[pl.ds(start, tk), :], start), 1.0, 0.0)
            return c + jnp.sum(m.reshape(tk // COUNT_ROWS, COUNT_ROWS, tq), axis=0)
        c = lax.fori_loop(0, nkb, body, jnp.zeros((COUNT_ROWS, tq), F32))
        return jnp.sum(c, axis=0, keepdims=True).astype(jnp.int32)

    def any_query(mask):
        return jnp.max(jnp.where(mask, 1.0, 0.0)) > 0.5

    few = (qpos + 1) <= topk
    lo0 = jnp.where(few, KEY_NINF, _ordered_key(smin))
    hi0 = _ordered_key(smax) + 1
    done0 = jnp.where(few, 1, 0)

    def bisect(st, by_value):
        lo, hi, done = st
        if by_value:
            mid = _ordered_key(0.5 * _unkey(lo) + 0.5 * _unkey(jnp.minimum(hi, -KEY_NINF)))
            mid = jnp.minimum(jnp.maximum(mid, lo + 1), hi - 1)
        else:
            mid = (lo >> 1) + (hi >> 1) + (lo & hi & 1)
        c = count(lambda blk, start: blk >= mid)
        active = (done == 0) & (hi - lo != 1)
        ge = c >= topk
        lo = jnp.where(active & ge, mid, lo)
        hi = jnp.where(active & jnp.logical_not(ge), mid, hi)
        done = jnp.where(active & (c == topk), 1, done)
        return lo, hi, done

    def bis_cond(st):
        lo, hi, done = st
        return any_query((done == 0) & (hi - lo != 1))

    def bis_body(st):
        return bisect(bisect(bisect(st, True), True), False)

    thr, _, done = lax.while_loop(bis_cond, bis_body, (lo0, hi0, done0))

    tied = (done == 0) & (thr > KEY_NINF)
    jcut0 = jnp.where(thr > KEY_NINF, jnp.int32(0x7FFFFFFF), jnp.int32(-1))

    def resolve_ties():
        need = topk - count(lambda blk, start: blk > thr)

        def step(b, ans):
            cand = ans | (jnp.int32(1) << (pos_bits - 1 - b))
            c = count(lambda blk, start: (blk == thr) & ((start + krow) < cand))
            return jnp.where(c < need, cand, ans)

        ans = lax.fori_loop(0, pos_bits, step, jnp.zeros((1, tq), jnp.int32))
        jcut_ref[...] = jnp.where(tied, ans, jcut0)

    jcut_ref[...] = jcut0
    pl.when(any_query(tied))(resolve_ties)
    jcut = jcut_ref[...]

    for u in range(tq // tqa):
        qs = slice(u * tqa, (u + 1) * tqa)
        thr_u, jcut_u = thr[:, qs], jcut[:, qs]
        nkb_u = (t0 + (u + 1) * tqa + tk - 1) // tk
        qg = [jnp.concatenate([q_ref[0, qs, (g * rep + r) * HEAD_DIM:(g * rep + r + 1) * HEAD_DIM]
                               for r in range(rep)], axis=0) for g in range(N_KV_HEADS)]
        acc_ref[...] = jnp.zeros_like(acc_ref)
        ncol = rep * tqa

        def col_reduce(op, x):
            part = op(x.reshape(tk // COUNT_ROWS, COUNT_ROWS, ncol), axis=0)
            return op(part, axis=0, keepdims=True)

        def score_into(s_ref, kb):
            start = pl.multiple_of(kb * tk, tk)
            key = key_ref[pl.ds(start, tk), qs]
            sel = (((key > thr_u) | ((key == thr_u) & ((start + krow) <= jcut_u)))
                   & (key > KEY_NINF))
            sel = jnp.concatenate([sel] * rep, axis=1)
            cmax = []
            for g in range(N_KV_HEADS):
                kblk = k_ref[0, pl.ds(start, tk), g * HEAD_DIM:(g + 1) * HEAD_DIM]
                s = lax.dot_general(kblk, qg[g], nt, preferred_element_type=F32)
                s = jnp.where(sel, s, -jnp.inf)
                s_ref[g] = s
                cmax.append(col_reduce(jnp.max, s))
            return cmax

        def accumulate(s_ref, kb, g, m_prev, m_cur):
            start = pl.multiple_of(kb * tk, tk)
            vtb = vt_ref[0, g * VT_ROWS:(g + 1) * VT_ROWS, pl.ds(start, tk)]
            m_safe = jnp.where(m_cur == -jnp.inf, 0.0, m_cur)
            p = jnp.exp2(s_ref[g] - m_safe)
            alpha = jnp.exp2(m_prev - m_safe)
            acc_ref[g] = alpha * acc_ref[g] + jnp.dot(vtb, p.astype(BF16),
                                                      preferred_element_type=F32)

        def step(s_cur, s_next, kb, st):
            cmax = score_into(s_next, kb + 1)
            new = []
            for g in range(N_KV_HEADS):
                m_prev, m_cur = st[2 * g:2 * g + 2]
                accumulate(s_cur, kb, g, m_prev, m_cur)
                new += [m_cur, jnp.maximum(m_cur, cmax[g])]
            return tuple(new)

        def finish(s_cur, kb, st):
            for g in range(N_KV_HEADS):
                accumulate(s_cur, kb, g, st[2 * g], st[2 * g + 1])

        cmax0 = score_into(sa_ref, 0)
        st0 = ()
        for g in range(N_KV_HEADS):
            st0 += (jnp.full((1, ncol), -jnp.inf, F32), cmax0[g])

        def pair(j, st):
            return step(sb_ref, sa_ref, 2 * j + 1, step(sa_ref, sb_ref, 2 * j, st))

        st = lax.fori_loop(0, (nkb_u - 1) // 2, pair, st0)
        last = nkb_u - 1

        @pl.when(last % 2 == 0)
        def _():
            finish(sa_ref, last, st)

        @pl.when(last % 2 == 1)
        def _():
            finish(sb_ref, last, step(sa_ref, sb_ref, last - 1, st))

        for g in range(N_KV_HEADS):
            acc = acc_ref[g]
            og = acc[:HEAD_DIM] / acc[HEAD_DIM:HEAD_DIM + 1]
            for r in range(rep):
                hd = g * rep + r
                o_ref[0, qs, hd * HEAD_DIM:(hd + 1) * HEAD_DIM] = (
                    og[:, r * tqa:(r + 1) * tqa].T.astype(BF16))


def _dsa(q, k, vt, qi, ki, wit, batch, seq):
    topk = min(TOPK_MAX, seq // 4)
    tq = DSA_TQ
    rep = N_HEADS // N_KV_HEADS
    r3 = lambda a: a.reshape(batch, seq, a.shape[-1])
    qrow = lambda width: pl.BlockSpec((1, tq, width), lambda b, i: (b, i, 0))
    full = lambda width: pl.BlockSpec((1, seq, width), lambda b, i: (b, 0, 0))
    out = pl.pallas_call(
        functools.partial(_dsa_kernel, topk=topk),
        out_shape=jax.ShapeDtypeStruct((batch, seq, N_HEADS * HEAD_DIM), BF16),
        grid=(batch, seq // tq),
        in_specs=[qrow(q.shape[-1]), qrow(qi.shape[-1]),
                  pl.BlockSpec((1, wit.shape[1], tq), lambda b, i: (b, 0, i)),
                  full(k.shape[-1]),
                  pl.BlockSpec((1, vt.shape[1], seq), lambda b, i: (b, 0, 0)),
                  full(ki.shape[-1])],
        out_specs=qrow(N_HEADS * HEAD_DIM),
        scratch_shapes=[pltpu.VMEM((seq, tq), jnp.int32),
                        pltpu.VMEM((N_KV_HEADS, VT_ROWS, rep * DSA_TQA), F32),
                        pltpu.VMEM((1, tq), jnp.int32),
                        pltpu.VMEM((N_KV_HEADS, DSA_TK, rep * DSA_TQA), F32),
                        pltpu.VMEM((N_KV_HEADS, DSA_TK, rep * DSA_TQA), F32)],
        compiler_params=_cparams("parallel", "arbitrary"),
        name="dsa_attention",
    )(r3(q), r3(qi), wit, r3(k), vt, r3(ki))
    return out.reshape(batch * seq, N_HEADS * HEAD_DIM)


def _out_ln_kernel(h_ref, a_ref, w_ref, g_ref, b_ref, o_ref):
    mix = jnp.dot(a_ref[...], w_ref[...], preferred_element_type=F32)
    o_ref[...] = _layer_norm(DN_ALPHA * h_ref[...] + mix, g_ref[...], b_ref[...])


def _out_ln(h, a, w, g, b):
    m, d = h.shape
    tm = PROJ_ROWS
    return pl.pallas_call(
        _out_ln_kernel,
        out_shape=jax.ShapeDtypeStruct((m, d), F32),
        grid=(m // tm,),
        in_specs=[pl.BlockSpec((tm, d), lambda i: (i, 0)),
                  pl.BlockSpec((tm, a.shape[1]), lambda i: (i, 0)),
                  _const_spec(w.shape), _const_spec(g.shape), _const_spec(b.shape)],
        out_specs=pl.BlockSpec((tm, d), lambda i: (i, 0)),
        compiler_params=_cparams("parallel"),
        name="c_out_proj_ln",
    )(h, a, w, g, b)


def kernel(x, p, positions, ln_g, ln_b, ffn_w1, ffn_w3, ffn_w2, ple_w_proj, ple_w_gate,
           ab_w_in, ab_w_out, conv_w, s5_lam_re, s5_lam_im, s5_log_dt, s5_b_re, s5_b_im,
           s5_c_re, s5_c_im, s5_d, s5_w_glu, s5_b_glu, c_w_in, c_w_out):
    batch, seq, d = x.shape
    m = batch * seq
    depth = ffn_w1.shape[0]
    g3 = ln_g.reshape(depth, 3, 1, d)
    b3 = ln_b.reshape(depth, 3, 1, d)
    wp = ple_w_proj.astype(BF16)
    wg = ple_w_gate.astype(BF16)
    p2 = p.reshape(depth, m, p.shape[-1])

    pos = positions
    ca, sa = _rope_tables(pos, HEAD_DIM // ROT_FRAC, HEAD_DIM, 1)
    ci, si = _rope_tables(pos, IDX_DIM // ROT_FRAC, IDX_DIM, LANES // IDX_DIM)
    ca, sa, ci, si = (t.reshape(m, LANES) for t in (ca, sa, ci, si))
    c_in = c_w_in.shape[-1]
    c_pad = -c_in % LANES
    cw_in = jnp.pad(c_w_in, ((0, 0), (0, 0), (0, c_pad))).astype(BF16)

    h = x.reshape(m, d)
    for i in range(depth):
        h = _ffn_ln(h, ffn_w1, ffn_w3, ffn_w2, i, 0, g3[i, 0], b3[i, 0])
        j = i // 2
        if i % 2 == 0:
            cg, u = _ab_proj(h, ab_w_in[j].astype(BF16))
            tables = _s5_tables(s5_lam_re[j], s5_lam_im[j], s5_log_dt[j], s5_b_re[j], s5_b_im[j],
                                s5_c_re[j], s5_c_im[j])
            y = _s5_linear(u, tables, batch, seq)
            h = _ab_mix(h, cg, u, y, conv_w[j], s5_d[j].reshape(1, -1), s5_w_glu[j].astype(BF16),
                        s5_b_glu[j].reshape(1, -1), ab_w_out[j].astype(BF16), g3[i, 1], b3[i, 1],
                        batch, seq)
        else:
            q, k, vt, qi, ki, wit = _c_proj(h, cw_in[j], ca, sa, ci, si, batch, seq)
            a = _dsa(q, k, vt, qi, ki, wit, batch, seq)
            h = _out_ln(h, a, c_w_out[j].astype(BF16), g3[i, 1], b3[i, 1])
        h = _ffn_ln(h, ffn_w1, ffn_w3, ffn_w2, i, 1, g3[i, 2], b3[i, 2],
                    ple=(p2[i], wp[i], wg[i]))
    return h.reshape(batch, seq, d)
```

```python
import functools
import math

import jax
import jax.numpy as jnp
from jax import lax
from jax.experimental import pallas as pl
from jax.experimental.pallas import tpu as pltpu

F32 = jnp.float32
BF16 = jnp.bfloat16
HIGHEST = lax.Precision.HIGHEST

DEPTH = 4
LN_EPS = 1e-5
DN_ALPHA = (2 * DEPTH) ** 0.25
CONV_WIDTH = 3
CONV_CH = 512
S5_CH = 512
S5_GROUP = 16
S5_GROUPS = S5_CH // S5_GROUP
S5_STATE = 64
N_HEADS = 8
HEAD_DIM = 128
N_KV_HEADS = 2
IDX_HEADS = 8
IDX_DIM = 64
TOPK_MAX = 256
ROPE_THETA = 500000.0
ROT_FRAC = 4

LANES = 128
SUBLANES = 8
VMEM_LIMIT = 60 * 1024 * 1024

FFN_ROWS = 512
FF_CHUNK = 256
PROJ_ROWS = 512
S5_T = 16
DSA_TQ = 256
DSA_TQA = 128
DSA_TK = 512
COUNT_ROWS = 64
BF16_ROWS = 16
VT_ROWS = HEAD_DIM + BF16_ROWS
QK_SCALE_LOG2 = HEAD_DIM ** -0.5 * math.log2(math.e)
KEY_NINF = -0x7F800000


def _cparams(*sem):
    return pltpu.CompilerParams(dimension_semantics=sem, vmem_limit_bytes=VMEM_LIMIT)


def _layer_norm(y, g, b):
    mu = jnp.mean(y, axis=-1, keepdims=True)
    d = y - mu
    var = jnp.mean(d * d, axis=-1, keepdims=True)
    return d * lax.rsqrt(var + LN_EPS) * g + b


def _const_spec(shape):
    nd = len(shape)
    return pl.BlockSpec(shape, lambda *_: (0,) * nd, pipeline_mode=pl.Buffered(1))


def _ffn_body(x_ref, w1_ref, w3_ref, w2_ref, g_ref, b_ref, acc_ref):
    x = x_ref[...]
    xb = x.astype(BF16)
    acc_ref[...] = jnp.zeros_like(acc_ref)
    for c in range(w1_ref.shape[1] // FF_CHUNK):
        cols = slice(c * FF_CHUNK, (c + 1) * FF_CHUNK)
        a = jnp.dot(xb, w1_ref[:, cols].astype(BF16), preferred_element_type=F32)
        gt = jnp.dot(xb, w3_ref[:, cols].astype(BF16), preferred_element_type=F32)
        act = (a * jax.nn.sigmoid(a)) * gt
        acc_ref[...] += jnp.dot(act.astype(BF16), w2_ref[cols, :].astype(BF16),
                                preferred_element_type=F32)
    y = DN_ALPHA * x + 0.5 * acc_ref[...]
    return _layer_norm(y, g_ref[...], b_ref[...])


def _ffn_kernel(x_ref, w1_ref, w3_ref, w2_ref, g_ref, b_ref, o_ref, acc_ref):
    o_ref[...] = _ffn_body(x_ref, w1_ref, w3_ref, w2_ref, g_ref, b_ref, acc_ref)


def _ffn_ple_kernel(x_ref, w1_ref, w3_ref, w2_ref, g_ref, b_ref, p_ref, wp_ref, wg_ref,
                    o_ref, acc_ref):
    h = _ffn_body(x_ref, w1_ref, w3_ref, w2_ref, g_ref, b_ref, acc_ref)
    gate = jax.nn.sigmoid(jnp.dot(h.astype(BF16), wg_ref[...], preferred_element_type=F32))
    pe = jnp.dot(p_ref[...].astype(BF16), wp_ref[...], preferred_element_type=F32)
    o_ref[...] = h + pe * gate


def _ffn_ln(x, w1, w3, w2, layer, half, g, b, ple=None):
    m, d = x.shape
    row = pl.BlockSpec((FFN_ROWS, d), lambda i: (i, 0))

    def whole(w):
        return pl.BlockSpec((None, None) + w.shape[2:], lambda i: (layer, half, 0, 0),
                            pipeline_mode=pl.Buffered(1))

    in_specs = [row, whole(w1), whole(w3), whole(w2), _const_spec((1, d)), _const_spec((1, d))]
    args = [x, w1, w3, w2, g, b]
    body = _ffn_kernel
    name = "ffn_ln"
    if ple is not None:
        p, wp, wg = ple
        in_specs += [pl.BlockSpec((FFN_ROWS, p.shape[1]), lambda i: (i, 0)),
                     _const_spec(wp.shape), _const_spec(wg.shape)]
        args += [p, wp, wg]
        body = _ffn_ple_kernel
        name = "ffn_ln_ple"
    return pl.pallas_call(
        body,
        out_shape=jax.ShapeDtypeStruct((m, d), F32),
        grid=(m // FFN_ROWS,),
        in_specs=in_specs,
        out_specs=row,
        scratch_shapes=[pltpu.VMEM((FFN_ROWS, d), F32)],
        compiler_params=_cparams("parallel"),
        name=name,
    )(*args)


def _store_chunks(u, uc_ref):
    nch = uc_ref.shape[0]
    width = S5_T * S5_GROUP
    u3 = u.reshape(nch, S5_T, u.shape[1])
    steps = [u3[:, t, :] for t in range(S5_T)]
    for g in range(S5_GROUPS):
        uc_ref[:, g * width:(g + 1) * width] = jnp.concatenate(
            [s[:, g * S5_GROUP:(g + 1) * S5_GROUP] for s in steps], axis=1)


def _rows_from_chunks(yc):
    nch = yc.shape[0]
    width = S5_T * S5_GROUP
    rows = [jnp.concatenate([yc[:, g * width + t * S5_GROUP:g * width + (t + 1) * S5_GROUP]
                             for g in range(S5_GROUPS)], axis=1) for t in range(S5_T)]
    return jnp.stack(rows, axis=1).reshape(nch * S5_T, S5_GROUPS * S5_GROUP)


def _ab_proj_kernel(x_ref, w_ref, cg_ref, u_ref, uc_ref):
    y = jnp.dot(x_ref[...].astype(BF16), w_ref[...], preferred_element_type=F32)
    n_cg = cg_ref.shape[1]
    cg_ref[...] = y[:, :n_cg]
    u = y[:, n_cg:]
    u_ref[...] = u
    _store_chunks(u, uc_ref)


def _ab_proj(x, w):
    m, d = x.shape
    n = w.shape[1]
    n_cg = 3 * CONV_CH
    tm = PROJ_ROWS
    return pl.pallas_call(
        _ab_proj_kernel,
        out_shape=(jax.ShapeDtypeStruct((m, n_cg), F32), jax.ShapeDtypeStruct((m, n - n_cg), F32),
                   jax.ShapeDtypeStruct((m // S5_T, S5_T * (n - n_cg)), F32)),
        grid=(m // tm,),
        in_specs=[pl.BlockSpec((tm, d), lambda i: (i, 0)), _const_spec((d, n))],
        out_specs=(pl.BlockSpec((tm, n_cg), lambda i: (i, 0)),
                   pl.BlockSpec((tm, n - n_cg), lambda i: (i, 0)),
                   pl.BlockSpec((tm // S5_T, S5_T * (n - n_cg)), lambda i: (i, 0))),
        compiler_params=_cparams("parallel"),
        name="ab_in_proj",
    )(x, w)


def _s5_tables(lam_re, lam_im, log_dt, b_re, b_im, c_re, c_im):
    lr = jnp.minimum(lam_re, -1e-4)
    li = lam_im
    dt = jnp.exp(log_dt)[:, None]
    mag = jnp.exp(lr * dt)
    ab_re = mag * jnp.cos(li * dt)
    ab_im = mag * jnp.sin(li * dt)
    nr, ni = ab_re - 1.0, ab_im
    den = lr * lr + li * li
    f_re = (nr * lr + ni * li) / den
    f_im = (ni * lr - nr * li) / den
    bb_re = f_re[..., None] * b_re - f_im[..., None] * b_im
    bb_im = f_re[..., None] * b_im + f_im[..., None] * b_re
    pw_re = [jnp.ones_like(ab_re)]
    pw_im = [jnp.zeros_like(ab_im)]
    for _ in range(S5_T):
        pr, pi = pw_re[-1], pw_im[-1]
        pw_re.append(pr * ab_re - pi * ab_im)
        pw_im.append(pr * ab_im + pi * ab_re)
    pw_re = jnp.stack(pw_re)
    pw_im = jnp.stack(pw_im)
    cp_re = c_re[None] * pw_re[:, :, None, :] - c_im[None] * pw_im[:, :, None, :]
    cp_im = c_re[None] * pw_im[:, :, None, :] + c_im[None] * pw_re[:, :, None, :]
    taps = (jnp.einsum('ngcp,gpd->ngcd', cp_re[:S5_T], bb_re, precision=HIGHEST)
            - jnp.einsum('ngcp,gpd->ngcd', cp_im[:S5_T], bb_im, precision=HIGHEST))
    s_idx = jnp.arange(S5_T)[:, None]
    t_idx = jnp.arange(S5_T)[None, :]
    lag = t_idx - s_idx
    k_st = taps[jnp.clip(lag, 0, S5_T - 1)]
    k_st = jnp.where((lag >= 0)[:, :, None, None, None], k_st, 0.0)
    toep = jnp.transpose(k_st, (2, 0, 4, 1, 3)).reshape(S5_GROUPS, S5_T * S5_GROUP, S5_T * S5_GROUP)
    rev_re = pw_re[S5_T - 1::-1][:S5_T]
    rev_im = pw_im[S5_T - 1::-1][:S5_T]
    win_re = rev_re[..., None] * bb_re[None] - rev_im[..., None] * bb_im[None]
    win_im = rev_re[..., None] * bb_im[None] + rev_im[..., None] * bb_re[None]
    w_in = jnp.concatenate([win_re, win_im], axis=2)
    w_in = jnp.transpose(w_in, (1, 0, 3, 2)).reshape(S5_GROUPS, S5_T * S5_GROUP, 2 * S5_STATE)
    wo_re = jnp.transpose(cp_re[1:], (1, 3, 0, 2))
    wo_im = jnp.transpose(-cp_im[1:], (1, 3, 0, 2))
    w_out = jnp.concatenate([wo_re, wo_im], axis=1).reshape(S5_GROUPS, 2 * S5_STATE, S5_T * S5_GROUP)
    a1 = jnp.concatenate([pw_re[S5_T], pw_re[S5_T]], axis=-1)
    a2 = jnp.concatenate([-pw_im[S5_T], pw_im[S5_T]], axis=-1)
    return toep, w_in, w_out, a1, a2


def _s5_local_kernel(uc_ref, win_ref, o_ref):
    o_ref[...] = jnp.dot(uc_ref[...].astype(BF16), win_ref[0].astype(BF16),
                         preferred_element_type=F32)


def _s5_local(uc, w_in):
    rows = uc.shape[0]
    g, width, ns = w_in.shape
    return pl.pallas_call(
        _s5_local_kernel,
        out_shape=jax.ShapeDtypeStruct((rows, g * ns), F32),
        grid=(g,),
        in_specs=[pl.BlockSpec((rows, width), lambda i: (0, i)),
                  pl.BlockSpec((1, width, ns), lambda i: (i, 0, 0))],
        out_specs=pl.BlockSpec((rows, ns), lambda i: (0, i)),
        compiler_params=_cparams("parallel"),
        name="s5_chunk_state_in",
    )(uc, w_in)


def _s5_scan_kernel(g_ref, a1_ref, a2_ref, o_ref):
    a1 = a1_ref[...]
    a2 = a2_ref[...]
    half = a1.shape[1] // 2

    a2s = pltpu.roll(a2, half, 1)

    def step(j, st):
        x, xs = st
        o_ref[j] = x
        g = g_ref[j]
        return a1 * x + a2 * xs + g, a1 * xs + a2s * x + pltpu.roll(g, half, 1)

    zero = jnp.zeros(a1.shape, F32)
    lax.fori_loop(0, g_ref.shape[0], step, (zero, zero), unroll=8)


def _s5_scan(gl, a1, a2):
    nch, rows, ns = gl.shape
    rb = 32
    return pl.pallas_call(
        _s5_scan_kernel,
        out_shape=jax.ShapeDtypeStruct((nch, rows, ns), F32),
        grid=(rows // rb,),
        in_specs=[pl.BlockSpec((nch, rb, ns), lambda i: (0, i, 0)),
                  pl.BlockSpec((rb, ns), lambda i: (i, 0)),
                  pl.BlockSpec((rb, ns), lambda i: (i, 0))],
        out_specs=pl.BlockSpec((nch, rb, ns), lambda i: (0, i, 0)),
        compiler_params=_cparams("parallel"),
        name="s5_chunk_scan",
    )(gl, a1, a2)


def _s5_out_kernel(uc_ref, xp_ref, toep_ref, wout_ref, o_ref):
    y = jnp.dot(uc_ref[...].astype(BF16), toep_ref[0].astype(BF16), preferred_element_type=F32)
    y = y + jnp.dot(xp_ref[...].astype(BF16), wout_ref[0].astype(BF16),
                    preferred_element_type=F32)
    o_ref[...] = y


def _s5_out(uc, xprev, toep, w_out):
    rows = uc.shape[0]
    g, ns, width = w_out.shape
    return pl.pallas_call(
        _s5_out_kernel,
        out_shape=jax.ShapeDtypeStruct((rows, g * width), F32),
        grid=(g,),
        in_specs=[pl.BlockSpec((rows, width), lambda i: (0, i)),
                  pl.BlockSpec((rows, ns), lambda i: (0, i)),
                  pl.BlockSpec((1, width, width), lambda i: (i, 0, 0)),
                  pl.BlockSpec((1, ns, width), lambda i: (i, 0, 0))],
        out_specs=pl.BlockSpec((rows, width), lambda i: (0, i)),
        compiler_params=_cparams("parallel"),
        name="s5_chunk_out",
    )(uc, xprev, toep, w_out)


def _s5_linear(uc, tables, batch, seq):
    toep, w_in, w_out, a1, a2 = tables
    nch = seq // S5_T
    ns = w_in.shape[2]
    gl = _s5_local(uc, w_in)
    gl = jnp.transpose(gl.reshape(batch, nch, S5_GROUPS * ns), (1, 0, 2))
    gl = gl.reshape(nch, batch * S5_GROUPS, ns)
    xprev = _s5_scan(gl, jnp.tile(a1, (batch, 1)), jnp.tile(a2, (batch, 1)))
    xprev = jnp.transpose(xprev.reshape(nch, batch, S5_GROUPS * ns), (1, 0, 2))
    xprev = xprev.reshape(batch * nch, S5_GROUPS * ns)
    return _s5_out(uc, xprev, toep, w_out)


def _gelu_tanh(x):
    c = math.sqrt(2.0 / math.pi)
    return 0.5 * x * (1.0 + jnp.tanh(c * (x + 0.044715 * (x * x * x))))


def _ab_mix_kernel(h_ref, cg_ref, u_ref, yc_ref, cw_ref, d_ref, wglu_ref, bglu_ref, wout_ref,
                   g_ref, b_ref, o_ref, uext_ref):
    tm = h_ref.shape[1]
    cc = CONV_CH
    halo = SUBLANES

    @pl.when(pl.program_id(1) == 0)
    def _():
        uext_ref[0:halo, :] = jnp.zeros((halo, cc), F32)

    cg = cg_ref[0]
    uc = cg[:, 2 * cc:3 * cc] * cg[:, 0:cc]
    uext_ref[halo:halo + tm, :] = uc
    v = (cw_ref[0:1, :] * uext_ref[halo - 2:halo - 2 + tm, :]
         + cw_ref[1:2, :] * uext_ref[halo - 1:halo - 1 + tm, :]
         + cw_ref[2:3, :] * uc)
    ya = cg[:, cc:2 * cc] * v
    uext_ref[0:halo, :] = uc[tm - halo:tm, :]

    y2 = _rows_from_chunks(yc_ref[0]) + d_ref[...] * u_ref[0]
    z = _gelu_tanh(y2)
    gate = jax.nn.sigmoid(jnp.dot(z.astype(BF16), wglu_ref[...], preferred_element_type=F32)
                          + bglu_ref[...])
    yb = z * gate
    mix = (jnp.dot(ya.astype(BF16), wout_ref[0:cc, :], preferred_element_type=F32)
           + jnp.dot(yb.astype(BF16), wout_ref[cc:, :], preferred_element_type=F32))
    o_ref[0] = _layer_norm(DN_ALPHA * h_ref[0] + mix, g_ref[...], b_ref[...])


def _ab_mix(h, cg, u, yc, conv_w, d_skip, w_glu, b_glu, w_out, g, b, batch, seq):
    d = h.shape[-1]
    tm = PROJ_ROWS

    def row(width):
        return pl.BlockSpec((1, tm, width), lambda bi, si: (bi, si, 0))

    r3 = lambda a: a.reshape(batch, -1, a.shape[-1])
    chunk_row = pl.BlockSpec((1, tm // S5_T, yc.shape[-1]), lambda bi, si: (bi, si, 0))
    out = pl.pallas_call(
        _ab_mix_kernel,
        out_shape=jax.ShapeDtypeStruct((batch, seq, d), F32),
        grid=(batch, seq // tm),
        in_specs=[row(d), row(3 * CONV_CH), row(S5_CH), chunk_row,
                  _const_spec(conv_w.shape), _const_spec(d_skip.shape), _const_spec(w_glu.shape),
                  _const_spec(b_glu.shape), _const_spec(w_out.shape),
                  _const_spec(g.shape), _const_spec(b.shape)],
        out_specs=row(d),
        scratch_shapes=[pltpu.VMEM((tm + SUBLANES, CONV_CH), F32)],
        compiler_params=_cparams("parallel", "arbitrary"),
        name="ab_mix_out_ln",
    )(r3(h), r3(cg), r3(u), r3(yc), conv_w, d_skip, w_glu, b_glu, w_out, g, b)
    return out.reshape(batch * seq, d)


def _rope_tables(positions, rot_dim, width, repeat):
    half = rot_dim // 2
    inv = ROPE_THETA ** (-jnp.arange(0, rot_dim, 2, dtype=F32) / rot_dim)
    ang = positions.astype(F32)[..., None] * inv
    cos, sin = jnp.cos(ang), jnp.sin(ang)
    lead = cos.shape[:-1]
    ones = jnp.ones(lead + (width - 2 * half,), F32)
    zeros = jnp.zeros(lead + (width - 2 * half,), F32)
    ct = jnp.concatenate([cos, cos, ones], axis=-1)
    st = jnp.concatenate([-sin, sin, zeros], axis=-1)
    return jnp.tile(ct, (1, 1, repeat)), jnp.tile(st, (1, 1, repeat))


def _rotate(t, cos_t, sin_t, half, period):
    lane = lax.broadcasted_iota(jnp.int32, t.shape, 1)
    first = (lane % period) < half
    partner = jnp.where(first, pltpu.roll(t, LANES - half, 1), pltpu.roll(t, half, 1))
    return t * cos_t + partner * sin_t


def _c_proj_kernel(x_ref, w_ref, ca_ref, sa_ref, ci_ref, si_ref,
                   q_ref, k_ref, vt_ref, qi_ref, ki_ref, wit_ref):
    y = jnp.dot(x_ref[...].astype(BF16), w_ref[...], preferred_element_type=F32)
    ca, sa, ci, si = ca_ref[...], sa_ref[...], ci_ref[...], si_ref[...]
    half_a = HEAD_DIM // ROT_FRAC // 2
    half_i = IDX_DIM // ROT_FRAC // 2
    nq = N_HEADS * HEAD_DIM
    nkv = N_KV_HEADS * HEAD_DIM
    nqi = IDX_HEADS * IDX_DIM
    for hd in range(N_HEADS):
        t = _rotate(y[:, hd * LANES:(hd + 1) * LANES], ca, sa, half_a, HEAD_DIM)
        q_ref[:, hd * LANES:(hd + 1) * LANES] = (t * QK_SCALE_LOG2).astype(BF16)
    for hd in range(N_KV_HEADS):
        t = y[:, nq + hd * LANES:nq + (hd + 1) * LANES]
        k_ref[:, hd * LANES:(hd + 1) * LANES] = _rotate(t, ca, sa, half_a, HEAD_DIM).astype(BF16)
    for hd in range(N_KV_HEADS):
        v = y[:, nq + nkv + hd * HEAD_DIM:nq + nkv + (hd + 1) * HEAD_DIM]
        vt_ref[0, hd * VT_ROWS:hd * VT_ROWS + HEAD_DIM, :] = v.T.astype(BF16)
        vt_ref[0, hd * VT_ROWS + HEAD_DIM:(hd + 1) * VT_ROWS, :] = jnp.ones(
            (BF16_ROWS, v.shape[0]), BF16)
    o = nq + 2 * nkv
    for pr in range(nqi // LANES):
        t = y[:, o + pr * LANES:o + (pr + 1) * LANES]
        qi_ref[:, pr * LANES:(pr + 1) * LANES] = _rotate(t, ci, si, half_i, IDX_DIM).astype(BF16)
    o += nqi
    t = y[:, o:o + LANES]
    lane = lax.broadcasted_iota(jnp.int32, t.shape, 1)
    ki_lo = jnp.where(lane < IDX_DIM, _rotate(t, ci, si, half_i, IDX_DIM), 0.0)
    ki_ref[:, 0:LANES] = ki_lo.astype(BF16)
    ki_ref[:, LANES:2 * LANES] = pltpu.roll(ki_lo, IDX_DIM, 1).astype(BF16)
    wit = t.T[IDX_DIM:IDX_DIM + IDX_HEADS, :]
    wit_ref[0] = wit * (IDX_HEADS ** -0.5 * IDX_DIM ** -0.5)


def _c_proj(x, w, ca, sa, ci, si, batch, seq):
    m, d = x.shape
    n = w.shape[1]
    tm = PROJ_ROWS
    spb = seq // tm
    nq = N_HEADS * HEAD_DIM
    nkv = N_KV_HEADS * HEAD_DIM
    nqi = IDX_HEADS * IDX_DIM
    row = lambda width: pl.BlockSpec((tm, width), lambda i: (i, 0))
    col = lambda height: pl.BlockSpec((1, height, tm), lambda i: (i // spb, 0, i % spb))
    return pl.pallas_call(
        _c_proj_kernel,
        out_shape=(jax.ShapeDtypeStruct((m, nq), BF16), jax.ShapeDtypeStruct((m, nkv), BF16),
                   jax.ShapeDtypeStruct((batch, N_KV_HEADS * VT_ROWS, seq), BF16),
                   jax.ShapeDtypeStruct((m, nqi), BF16),
                   jax.ShapeDtypeStruct((m, 2 * LANES), BF16),
                   jax.ShapeDtypeStruct((batch, IDX_HEADS, seq), F32)),
        grid=(m // tm,),
        in_specs=[row(d), _const_spec((d, n)), row(LANES), row(LANES), row(LANES), row(LANES)],
        out_specs=(row(nq), row(nkv), col(N_KV_HEADS * VT_ROWS), row(nqi), row(2 * LANES),
                   col(IDX_HEADS)),
        compiler_params=_cparams("parallel"),
        name="c_in_proj_rope",
    )(x, w, ca, sa, ci, si)


def _ordered_key(s):
    bits = pltpu.bitcast(s, jnp.int32)
    return jnp.where(bits < 0, -(bits ^ jnp.int32(-0x80000000)), bits)


def _unkey(k):
    return pltpu.bitcast(jnp.where(k < 0, (-k) ^ jnp.int32(-0x80000000), k), F32)


def _dsa_kernel(q_ref, qi_ref, wit_ref, k_ref, vt_ref, ki_ref, o_ref,
                key_ref, acc_ref, jcut_ref, sa_ref, sb_ref, *, topk):
    tq, tk, tqa = DSA_TQ, DSA_TK, DSA_TQA
    pos_bits = (k_ref.shape[1] - 1).bit_length()
    rep = N_HEADS // N_KV_HEADS
    t0 = pl.program_id(1) * tq
    nkb = (t0 + tq + tk - 1) // tk
    qpos = t0 + lax.broadcasted_iota(jnp.int32, (1, tq), 1)
    krow = lax.broadcasted_iota(jnp.int32, (tk, 1), 0)
    nt = (((1,), (1,)), ((), ()))

    qi = qi_ref[0]
    wit = wit_ref[0]

    def score_block(kb, carry):
        smax, smin = carry
        start = pl.multiple_of(kb * tk, tk)
        sc = jnp.zeros((tk, tq), F32)
        for hd in range(IDX_HEADS):
            pair, side = divmod(hd, LANES // IDX_DIM)
            kib = ki_ref[0, pl.ds(start, tk), side * LANES:(side + 1) * LANES]
            lg = lax.dot_general(kib, qi[:, pair * LANES:(pair + 1) * LANES], nt,
                                 preferred_element_type=F32)
            sc = sc + jnp.maximum(lg, 0.0) * wit[hd:hd + 1, :]
        causal = (start + krow) <= qpos
        key_ref[pl.ds(start, tk), :] = jnp.where(causal, _ordered_key(sc), KEY_NINF)
        smax = jnp.maximum(smax, jnp.max(jnp.where(causal, sc, -jnp.inf), axis=0, keepdims=True))
        smin = jnp.minimum(smin, jnp.min(jnp.where(causal, sc, jnp.inf), axis=0, keepdims=True))
        return smax, smin

    smax, smin = lax.fori_loop(0, nkb, score_block,
                               (jnp.full((1, tq), -jnp.inf, F32), jnp.full((1, tq), jnp.inf, F32)))

    def count(pred):
        def body(kb, c):
            start = pl.multiple_of(kb * tk, tk)
            m = jnp.where(pred(key_ref[pl.ds(start, tk), :], start), 1.0, 0.0)
            return c + jnp.sum(m.reshape(tk // COUNT_ROWS, COUNT_ROWS, tq), axis=0)
        c = lax.fori_loop(0, nkb, body, jnp.zeros((COUNT_ROWS, tq), F32))
        return jnp.sum(c, axis=0, keepdims=True).astype(jnp.int32)

    def any_query(mask):
        return jnp.max(jnp.where(mask, 1.0, 0.0)) > 0.5

    few = (qpos + 1) <= topk
    lo0 = jnp.where(few, KEY_NINF, _ordered_key(smin))
    hi0 = _ordered_key(smax) + 1
    done0 = jnp.where(few, 1, 0)

    def bisect(st, by_value):
        lo, hi, done = st
        if by_value:
            mid = _ordered_key(0.5 * _unkey(lo) + 0.5 * _unkey(jnp.minimum(hi, -KEY_NINF)))
            mid = jnp.minimum(jnp.maximum(mid, lo + 1), hi - 1)
        else:
            mid = (lo >> 1) + (hi >> 1) + (lo & hi & 1)
        c = count(lambda blk, start: blk >= mid)
        active = (done == 0) & (hi - lo != 1)
        ge = c >= topk
        lo = jnp.where(active & ge, mid, lo)
        hi = jnp.where(active & jnp.logical_not(ge), mid, hi)
        done = jnp.where(active & (c == topk), 1, done)
        return lo, hi, done

    def bis_cond(st):
        lo, hi, done = st
        return any_query((done == 0) & (hi - lo != 1))

    def bis_body(st):
        return bisect(bisect(bisect(st, True), True), False)

    thr, _, done = lax.while_loop(bis_cond, bis_body, (lo0, hi0, done0))

    tied = (done == 0) & (thr > KEY_NINF)
    jcut0 = jnp.where(thr > KEY_NINF, jnp.int32(0x7FFFFFFF), jnp.int32(-1))

    def resolve_ties():
        need = topk - count(lambda blk, start: blk > thr)

        def step(b, ans):
            cand = ans | (jnp.int32(1) << (pos_bits - 1 - b))
            c = count(lambda blk, start: (blk == thr) & ((start + krow) < cand))
            return jnp.where(c < need, cand, ans)

        ans = lax.fori_loop(0, pos_bits, step, jnp.zeros((1, tq), jnp.int32))
        jcut_ref[...] = jnp.where(tied, ans, jcut0)

    jcut_ref[...] = jcut0
    pl.when(any_query(tied))(resolve_ties)
    jcut = jcut_ref[...]

    for u in range(tq // tqa):
        qs = slice(u * tqa, (u + 1) * tqa)
        thr_u, jcut_u = thr[:, qs], jcut[:, qs]
        nkb_u = (t0 + (u + 1) * tqa + tk - 1) // tk
        qg = [jnp.concatenate([q_ref[0, qs, (g * rep + r) * HEAD_DIM:(g * rep + r + 1) * HEAD_DIM]
                               for r in range(rep)], axis=0) for g in range(N_KV_HEADS)]
        acc_ref[...] = jnp.zeros_like(acc_ref)
        ncol = rep * tqa

        def col_reduce(op, x):
            part = op(x.reshape(tk // COUNT_ROWS, COUNT_ROWS, ncol), axis=0)
            return op(part, axis=0, keepdims=True)

        def score_into(s_ref, kb):
            start = pl.multiple_of(kb * tk, tk)
            key = key_ref[pl.ds(start, tk), qs]
            sel = (((key > thr_u) | ((key == thr_u) & ((start + krow) <= jcut_u)))
                   & (key > KEY_NINF))
            sel = jnp.concatenate([sel] * rep, axis=1)
            cmax = []
            for g in range(N_KV_HEADS):
                kblk = k_ref[0, pl.ds(start, tk), g * HEAD_DIM:(g + 1) * HEAD_DIM]
                s = lax.dot_general(kblk, qg[g], nt, preferred_element_type=F32)
                s = jnp.where(sel, s, -jnp.inf)
                s_ref[g] = s
                cmax.append(col_reduce(jnp.max, s))
            return cmax

        def accumulate(s_ref, kb, g, m_prev, m_cur):
            start = pl.multiple_of(kb * tk, tk)
            vtb = vt_ref[0, g * VT_ROWS:(g + 1) * VT_ROWS, pl.ds(start, tk)]
            m_safe = jnp.where(m_cur == -jnp.inf, 0.0, m_cur)
            p = jnp.exp2(s_ref[g] - m_safe)
            alpha = jnp.exp2(m_prev - m_safe)
            acc_ref[g] = alpha * acc_ref[g] + jnp.dot(vtb, p.astype(BF16),
                                                      preferred_element_type=F32)

        def step(s_cur, s_next, kb, st):
            cmax = score_into(s_next, kb + 1)
            new = []
            for g in range(N_KV_HEADS):
                m_prev, m_cur = st[2 * g:2 * g + 2]
                accumulate(s_cur, kb, g, m_prev, m_cur)
                new += [m_cur, jnp.maximum(m_cur, cmax[g])]
            return tuple(new)

        def finish(s_cur, kb, st):
            for g in range(N_KV_HEADS):
                accumulate(s_cur, kb, g, st[2 * g], st[2 * g + 1])

        cmax0 = score_into(sa_ref, 0)
        st0 = ()
        for g in range(N_KV_HEADS):
            st0 += (jnp.full((1, ncol), -jnp.inf, F32), cmax0[g])

        def pair(j, st):
            return step(sb_ref, sa_ref, 2 * j + 1, step(sa_ref, sb_ref, 2 * j, st))

        st = lax.fori_loop(0, (nkb_u - 1) // 2, pair, st0)
        last = nkb_u - 1

        @pl.when(last % 2 == 0)
        def _():
            finish(sa_ref, last, st)

        @pl.when(last % 2 == 1)
        def _():
            finish(sb_ref, last, step(sa_ref, sb_ref, last - 1, st))

        for g in range(N_KV_HEADS):
            acc = acc_ref[g]
            og = acc[:HEAD_DIM] / acc[HEAD_DIM:HEAD_DIM + 1]
            for r in range(rep):
                hd = g * rep + r
                o_ref[0, qs, hd * HEAD_DIM:(hd + 1) * HEAD_DIM] = (
                    og[:, r * tqa:(r + 1) * tqa].T.astype(BF16))


def _dsa(q, k, vt, qi, ki, wit, batch, seq):
    topk = min(TOPK_MAX, seq // 4)
    tq = DSA_TQ
    rep = N_HEADS // N_KV_HEADS
    r3 = lambda a: a.reshape(batch, seq, a.shape[-1])
    qrow = lambda width: pl.BlockSpec((1, tq, width), lambda b, i: (b, i, 0))
    full = lambda width: pl.BlockSpec((1, seq, width), lambda b, i: (b, 0, 0))
    out = pl.pallas_call(
        functools.partial(_dsa_kernel, topk=topk),
        out_shape=jax.ShapeDtypeStruct((batch, seq, N_HEADS * HEAD_DIM), BF16),
        grid=(batch, seq // tq),
        in_specs=[qrow(q.shape[-1]), qrow(qi.shape[-1]),
                  pl.BlockSpec((1, wit.shape[1], tq), lambda b, i: (b, 0, i)),
                  full(k.shape[-1]),
                  pl.BlockSpec((1, vt.shape[1], seq), lambda b, i: (b, 0, 0)),
                  full(ki.shape[-1])],
        out_specs=qrow(N_HEADS * HEAD_DIM),
        scratch_shapes=[pltpu.VMEM((seq, tq), jnp.int32),
                        pltpu.VMEM((N_KV_HEADS, VT_ROWS, rep * DSA_TQA), F32),
                        pltpu.VMEM((1, tq), jnp.int32),
                        pltpu.VMEM((N_KV_HEADS, DSA_TK, rep * DSA_TQA), F32),
                        pltpu.VMEM((N_KV_HEADS, DSA_TK, rep * DSA_TQA), F32)],
        compiler_params=_cparams("parallel", "arbitrary"),
        name="dsa_attention",
    )(r3(q), r3(qi), wit, r3(k), vt, r3(ki))
    return out.reshape(batch * seq, N_HEADS * HEAD_DIM)


def _out_ln_kernel(h_ref, a_ref, w_ref, g_ref, b_ref, o_ref):
    mix = jnp.dot(a_ref[...], w_ref[...], preferred_element_type=F32)
    o_ref[...] = _layer_norm(DN_ALPHA * h_ref[...] + mix, g_ref[...], b_ref[...])


def _out_ln(h, a, w, g, b):
    m, d = h.shape
    tm = PROJ_ROWS
    return pl.pallas_call(
        _out_ln_kernel,
        out_shape=jax.ShapeDtypeStruct((m, d), F32),
        grid=(m // tm,),
        in_specs=[pl.BlockSpec((tm, d), lambda i: (i, 0)),
                  pl.BlockSpec((tm, a.shape[1]), lambda i: (i, 0)),
                  _const_spec(w.shape), _const_spec(g.shape), _const_spec(b.shape)],
        out_specs=pl.BlockSpec((tm, d), lambda i: (i, 0)),
        compiler_params=_cparams("parallel"),
        name="c_out_proj_ln",
    )(h, a, w, g, b)


def kernel(x, p, positions, ln_g, ln_b, ffn_w1, ffn_w3, ffn_w2, ple_w_proj, ple_w_gate,
           ab_w_in, ab_w_out, conv_w, s5_lam_re, s5_lam_im, s5_log_dt, s5_b_re, s5_b_im,
           s5_c_re, s5_c_im, s5_d, s5_w_glu, s5_b_glu, c_w_in, c_w_out):
    batch, seq, d = x.shape
    m = batch * seq
    depth = ffn_w1.shape[0]
    g3 = ln_g.reshape(depth, 3, 1, d)
    b3 = ln_b.reshape(depth, 3, 1, d)
    wp = ple_w_proj.astype(BF16)
    wg = ple_w_gate.astype(BF16)
    p2 = p.reshape(depth, m, p.shape[-1])

    pos = positions
    ca, sa = _rope_tables(pos, HEAD_DIM // ROT_FRAC, HEAD_DIM, 1)
    ci, si = _rope_tables(pos, IDX_DIM // ROT_FRAC, IDX_DIM, LANES // IDX_DIM)
    ca, sa, ci, si = (t.reshape(m, LANES) for t in (ca, sa, ci, si))
    c_in = c_w_in.shape[-1]
    c_pad = -c_in % LANES
    cw_in = jnp.pad(c_w_in, ((0, 0), (0, 0), (0, c_pad))).astype(BF16)

    h = x.reshape(m, d)
    for i in range(depth):
        h = _ffn_ln(h, ffn_w1, ffn_w3, ffn_w2, i, 0, g3[i, 0], b3[i, 0])
        j = i // 2
        if i % 2 == 0:
            cg, u, uc = _ab_proj(h, ab_w_in[j].astype(BF16))
            tables = _s5_tables(s5_lam_re[j], s5_lam_im[j], s5_log_dt[j], s5_b_re[j], s5_b_im[j],
                                s5_c_re[j], s5_c_im[j])
            yc = _s5_linear(uc, tables, batch, seq)
            h = _ab_mix(h, cg, u, yc, conv_w[j], s5_d[j].reshape(1, -1), s5_w_glu[j].astype(BF16),
                        s5_b_glu[j].reshape(1, -1), ab_w_out[j].astype(BF16), g3[i, 1], b3[i, 1],
                        batch, seq)
        else:
            q, k, vt, qi, ki, wit = _c_proj(h, cw_in[j], ca, sa, ci, si, batch, seq)
            a = _dsa(q, k, vt, qi, ki, wit, batch, seq)
            h = _out_ln(h, a, c_w_out[j].astype(BF16), g3[i, 1], b3[i, 1])
        h = _ffn_ln(h, ffn_w1, ffn_w3, ffn_w2, i, 1, g3[i, 2], b3[i, 2],
                    ple=(p2[i], wp[i], wg[i]))
    return h.reshape(batch, seq, d)
```
